```python
import math
import jax
import jax.numpy as jnp
from jax import lax
import numpy as np

D_MODEL = 1024
BATCH = 8
SEQ = 2048
DEPTH = 2
DEC_BATCH = 32
DEC_SEQ = 1
PAST_LEN = 16384
PAGE_SIZE = 128

W_LRU = D_MODEL
LRU_HEADS = 16
LRU_BLK = W_LRU // LRU_HEADS
CONV_A = 4
RG_C = 8.0
W_SC = D_MODEL
CONV_B = 3
W_IN0 = 2 * W_LRU + 4 * W_SC
SPLIT0 = [W_LRU, 2 * W_LRU, 2 * W_LRU + W_SC, 2 * W_LRU + 2 * W_SC, 2 * W_LRU + 3 * W_SC]
N_HEADS = 8
HEAD_DIM = D_MODEL // (2 * N_HEADS)
V_DIM = 2 * HEAD_DIM
W_QK = 2 * N_HEADS * HEAD_DIM
W_ATT = N_HEADS * V_DIM
W_IN1 = 2 * W_QK + 2 * W_ATT
ROT_DIM = HEAD_DIM // 4
ROPE_THETA = 500000.0
LAMBDA_INIT = 0.8 - 0.6 * math.exp(-0.3 * 1)
SCALE = HEAD_DIM ** -0.5
Q_BLK = 128
PAGES_PER_STEP = 16
EPS = 1e-6
NEG = -1e30

kernel_name = 'hybrid_rglru_shortconv_diffattn_step'


def rms_norm(x, g):
    xf = x.astype(jnp.float32)
    y = xf * lax.rsqrt(jnp.mean(jnp.square(xf), axis=-1, keepdims=True) + EPS)
    return (y * g.astype(jnp.float32)).astype(x.dtype)


def ada_modulation(c, w, b):
    m = jax.nn.silu(c) @ w + b
    shift, scale, gate = jnp.split(m[:, None, :], 3, axis=-1)
    return shift, scale, gate


def modulated_norm(x, g, shift, scale):
    return rms_norm(x, g) * (1.0 + scale) + shift


def causal_dwconv(u, buf, w):
    k_w = w.shape[0]
    t = u.shape[1]
    full = jnp.concatenate([buf.astype(u.dtype), u], axis=1)
    y = full[:, 0:t] * w[0]
    for j in range(1, k_w):
        y = y + full[:, j:j + t] * w[j]
    return y, full[:, t:]


def rg_lru(x, h0, w_r, b_r, w_i, b_i, lam):
    b_, t, _ = x.shape
    xf = x.astype(jnp.float32)
    xb = xf.reshape(b_, t, LRU_HEADS, LRU_BLK)
    r = jax.nn.sigmoid(jnp.einsum('btnc,ncd->btnd', xb, w_r.astype(jnp.float32)).reshape(b_, t, W_LRU) + b_r.astype(jnp.float32))
    i = jax.nn.sigmoid(jnp.einsum('btnc,ncd->btnd', xb, w_i.astype(jnp.float32)).reshape(b_, t, W_LRU) + b_i.astype(jnp.float32))
    log_a = -RG_C * r * jax.nn.softplus(-lam.astype(jnp.float32))
    a = jnp.exp(log_a)
    u = jnp.sqrt(-jnp.expm1(2.0 * log_a)) * (i * xf)

    def step(h, au):
        a_t, u_t = au
        h = a_t * h + u_t
        return h, h

    h_last, hs = lax.scan(step, h0.astype(jnp.float32), (jnp.swapaxes(a, 0, 1), jnp.swapaxes(u, 0, 1)))
    return jnp.swapaxes(hs, 0, 1), h_last


def recurrent_conv_mixer(h, lru_h0, conv_a_buf, conv_b_buf, w_in, conv_a_w, conv_a_b, lru_wr, lru_br,
                         lru_wi, lru_bi, lru_lam, conv_b_w, w_out):
    z = h @ w_in
    xa, ga, gate_b, gate_c, hb, gb = jnp.split(z, SPLIT0, axis=-1)
    xa_c, conv_a_new = causal_dwconv(xa, conv_a_buf, conv_a_w)
    hs, h_last = rg_lru(xa_c + conv_a_b, lru_h0, lru_wr, lru_br, lru_wi, lru_bi, lru_lam)
    y_a = hs.astype(h.dtype) * jax.nn.silu(ga)
    u = gate_c * hb
    uc, conv_b_new = causal_dwconv(u, conv_b_buf, conv_b_w)
    y_b = gate_b * uc * jax.nn.silu(gb)
    out = jnp.concatenate([y_a, y_b], axis=-1) @ w_out
    return out, h_last, conv_a_new, conv_b_new


def rope_partial(x, pos):
    inv = jnp.power(jnp.float32(ROPE_THETA), -jnp.arange(0, ROT_DIM, 2, dtype=jnp.float32) / ROT_DIM)
    ang = pos.astype(jnp.float32)[:, None] * inv[None, :]
    cos = jnp.cos(ang)[None, :, None, :]
    sin = jnp.sin(ang)[None, :, None, :]
    xr = x[..., :ROT_DIM].astype(jnp.float32)
    x1, x2 = xr[..., :ROT_DIM // 2], xr[..., ROT_DIM // 2:]
    rot = jnp.concatenate([x1 * cos - x2 * sin, x2 * cos + x1 * sin], axis=-1)
    return jnp.concatenate([rot.astype(x.dtype), x[..., ROT_DIM:]], axis=-1)


def diff_attn_qkvg(h, pos, w_in):
    b_, t, _ = h.shape
    z = h @ w_in
    q, k, v, g = jnp.split(z, [W_QK, 2 * W_QK, 2 * W_QK + W_ATT], axis=-1)
    q = rope_partial(q.reshape(b_, t, 2 * N_HEADS, HEAD_DIM), pos)
    k = rope_partial(k.reshape(b_, t, 2 * N_HEADS, HEAD_DIM), pos)
    v = v.reshape(b_, t, N_HEADS, V_DIM)
    return q, k, v, g


def diff_lambda(lq1, lk1, lq2, lk2):
    f = jnp.float32
    return (jnp.exp(jnp.sum(lq1.astype(f) * lk1.astype(f))) - jnp.exp(jnp.sum(lq2.astype(f) * lk2.astype(f)))
            + LAMBDA_INIT)


def diff_attn_prompt(q, k, v, lam):
    b_, s_len = q.shape[:2]
    n_blk = s_len // Q_BLK
    qb = jnp.swapaxes(q.reshape(b_, n_blk, Q_BLK, 2 * N_HEADS, HEAD_DIM), 0, 1)
    kf = k.astype(jnp.float32)
    vf = v.astype(jnp.float32)
    k_pos = jnp.arange(s_len)

    def one_block(args):
        q_blk, blk = args
        s = jnp.einsum('bqnd,bknd->bnqk', q_blk.astype(jnp.float32) * SCALE, kf)
        q_pos = blk * Q_BLK + jnp.arange(Q_BLK)
        s = jnp.where(k_pos[None, :] <= q_pos[:, None], s, NEG)
        p = jax.nn.softmax(s, axis=-1).reshape(b_, N_HEADS, 2, Q_BLK, s_len)
        w = p[:, :, 0] - lam * p[:, :, 1]
        return jnp.einsum('bhqk,bkhe->bqhe', w, vf)

    o = lax.map(one_block, (qb, jnp.arange(n_blk)))
    return jnp.swapaxes(o, 0, 1).reshape(b_, s_len, N_HEADS, V_DIM)


def diff_attn_sample(q, k_new, v_new, cache_k, cache_v, page_table, lam):
    db, t = q.shape[:2]
    qf = q.astype(jnp.float32) * SCALE
    s = jnp.einsum('btnd,bsnd->bnts', qf, k_new.astype(jnp.float32)).reshape(db, N_HEADS, 2, t, t)
    causal = jnp.arange(t)[None, :] <= jnp.arange(t)[:, None]
    s = jnp.where(causal, s, NEG)
    m = jnp.max(s, axis=-1)
    p = jnp.exp(s - m[..., None])
    l = jnp.sum(p, axis=-1)
    acc = jnp.einsum('bhcts,bshe->bhcte', p, v_new.astype(jnp.float32))
    n_pages = page_table.shape[1]
    g = math.gcd(n_pages, PAGES_PER_STEP)
    pt = jnp.swapaxes(page_table.reshape(db, n_pages // g, g), 0, 1)

    def step(carry, pids):
        m, l, acc = carry
        kb = cache_k[pids].reshape(db, g * PAGE_SIZE, 2 * N_HEADS, HEAD_DIM).astype(jnp.float32)
        vb = cache_v[pids].reshape(db, g * PAGE_SIZE, N_HEADS, V_DIM).astype(jnp.float32)
        s = jnp.einsum('btnd,bknd->bntk', qf, kb).reshape(db, N_HEADS, 2, t, g * PAGE_SIZE)
        m_new = jnp.maximum(m, jnp.max(s, axis=-1))
        corr = jnp.exp(m - m_new)
        p = jnp.exp(s - m_new[..., None])
        l = l * corr + jnp.sum(p, axis=-1)
        acc = acc * corr[..., None] + jnp.einsum('bhctk,bkhe->bhcte', p, vb)
        return (m_new, l, acc), None

    (m, l, acc), _ = lax.scan(step, (m, l, acc), pt)
    o_maps = acc / l[..., None]
    o = o_maps[:, :, 0] - lam * o_maps[:, :, 1]
    return jnp.transpose(o, (0, 2, 1, 3))


def diff_attn_output(o, g, subln_g, w_out):
    b_, t = o.shape[:2]
    o = rms_norm(o, subln_g) * (1.0 - LAMBDA_INIT)
    o = o.reshape(b_, t, W_ATT).astype(g.dtype)
    return (o * jax.nn.silu(g)) @ w_out


def setup_inputs(seed: int = 0) -> dict:
    key = jax.random.key(seed)
    ks = iter(jax.random.split(key, 48))
    f32 = jnp.float32

    def nrm(shape, scale):
        return jax.random.normal(next(ks), shape, f32) * scale

    n_pages = PAST_LEN // PAGE_SIZE
    n_used = DEC_BATCH * n_pages
    n_pool = n_used + n_used // 4
    perm = jax.random.permutation(next(ks), n_pool)
    page_table = perm[:n_used].reshape(DEC_BATCH, n_pages).astype(jnp.int32)
    a8 = jax.random.uniform(next(ks), (W_LRU,), f32, 0.9, 0.999)
    a0 = a8 ** (1.0 / RG_C)
    lru_lam = jnp.log(a0) - jnp.log1p(-a0)
    d = D_MODEL
    return {
        'x_prompt': nrm((BATCH, SEQ, d), 1.0),
        'x_sample': nrm((DEC_BATCH, DEC_SEQ, d), 1.0),
        'state_lru_h': nrm((DEC_BATCH, W_LRU), 0.5),
        'state_conv_a': nrm((DEC_BATCH, CONV_A - 1, W_LRU), 1.0),
        'state_conv_b': nrm((DEC_BATCH, CONV_B - 1, W_SC), 1.0),
        'cache_k': nrm((n_pool, PAGE_SIZE, 2 * N_HEADS, HEAD_DIM), 1.0),
        'cache_v': nrm((n_pool, PAGE_SIZE, N_HEADS, V_DIM), 1.0),
        'page_table': page_table,
        'c_prompt': nrm((BATCH, d), 1.0),
        'c_sample': nrm((DEC_BATCH, d), 1.0),
        'l0_norm_g': 1.0 + nrm((d,), 0.02),
        'l0_ada_w': nrm((d, 3 * d), d ** -0.5),
        'l0_ada_b': nrm((3 * d,), 0.01),
        'l0_w_in': nrm((d, W_IN0), d ** -0.5),
        'l0_conv_a_w': nrm((CONV_A, W_LRU), CONV_A ** -0.5),
        'l0_conv_a_b': nrm((W_LRU,), 0.01),
        'l0_lru_wr': nrm((LRU_HEADS, LRU_BLK, LRU_BLK), LRU_BLK ** -0.5),
        'l0_lru_br': nrm((W_LRU,), 0.01),
        'l0_lru_wi': nrm((LRU_HEADS, LRU_BLK, LRU_BLK), LRU_BLK ** -0.5),
        'l0_lru_bi': nrm((W_LRU,), 0.01),
        'l0_lru_lam': lru_lam,
        'l0_conv_b_w': nrm((CONV_B, W_SC), CONV_B ** -0.5),
        'l0_w_out': nrm((W_LRU + W_SC, d), (W_LRU + W_SC) ** -0.5),
        'l1_norm_g': 1.0 + nrm((d,), 0.02),
        'l1_ada_w': nrm((d, 3 * d), d ** -0.5),
        'l1_ada_b': nrm((3 * d,), 0.01),
        'l1_w_in': nrm((d, W_IN1), d ** -0.5),
        'l1_lam_q1': nrm((HEAD_DIM,), 0.1),
        'l1_lam_k1': nrm((HEAD_DIM,), 0.1),
        'l1_lam_q2': nrm((HEAD_DIM,), 0.1),
        'l1_lam_k2': nrm((HEAD_DIM,), 0.1),
        'l1_subln_g': 1.0 + nrm((V_DIM,), 0.02),
        'l1_w_out': nrm((W_ATT, d), W_ATT ** -0.5),
        'final_norm_g': 1.0 + nrm((d,), 0.02),
    }


def reference(x_prompt, x_sample, state_lru_h, state_conv_a, state_conv_b, cache_k, cache_v, page_table,
              c_prompt, c_sample, l0_norm_g, l0_ada_w, l0_ada_b, l0_w_in, l0_conv_a_w, l0_conv_a_b,
              l0_lru_wr, l0_lru_br, l0_lru_wi, l0_lru_bi, l0_lru_lam, l0_conv_b_w, l0_w_out,
              l1_norm_g, l1_ada_w, l1_ada_b, l1_w_in, l1_lam_q1, l1_lam_k1, l1_lam_q2, l1_lam_k2,
              l1_subln_g, l1_w_out, final_norm_g):
    n_pages = page_table.shape[1]
    past_len = n_pages * PAGE_SIZE
    b, s_len = x_prompt.shape[:2]
    t_len = x_sample.shape[1]
    pos_p = jnp.arange(s_len, dtype=jnp.int32)
    pos_s = past_len + jnp.arange(t_len, dtype=jnp.int32)
    w_rec = (l0_w_in, l0_conv_a_w, l0_conv_a_b, l0_lru_wr, l0_lru_br, l0_lru_wi, l0_lru_bi, l0_lru_lam,
             l0_conv_b_w, l0_w_out)
    xp, xs = x_prompt, x_sample
    for layer in range(DEPTH):
        if layer % 2 == 0:
            shp, scp, gtp = ada_modulation(c_prompt, l0_ada_w, l0_ada_b)
            shs, scs, gts = ada_modulation(c_sample, l0_ada_w, l0_ada_b)
            hp = modulated_norm(xp, l0_norm_g, shp, scp)
            hs = modulated_norm(xs, l0_norm_g, shs, scs)
            zero_h = jnp.zeros((b, W_LRU), jnp.float32)
            zero_a = jnp.zeros((b, CONV_A - 1, W_LRU), xp.dtype)
            zero_b = jnp.zeros((b, CONV_B - 1, W_SC), xp.dtype)
            op, lru_h_p, conv_a_p, conv_b_p = recurrent_conv_mixer(hp, zero_h, zero_a, zero_b, *w_rec)
            os_, lru_h_s, conv_a_s, conv_b_s = recurrent_conv_mixer(hs, state_lru_h, state_conv_a,
                                                                    state_conv_b, *w_rec)
            xp = xp + gtp * op
            xs = xs + gts * os_
        else:
            shp, scp, gtp = ada_modulation(c_prompt, l1_ada_w, l1_ada_b)
            shs, scs, gts = ada_modulation(c_sample, l1_ada_w, l1_ada_b)
            hp = modulated_norm(xp, l1_norm_g, shp, scp)
            hs = modulated_norm(xs, l1_norm_g, shs, scs)
            lam = diff_lambda(l1_lam_q1, l1_lam_k1, l1_lam_q2, l1_lam_k2)
            qp, kp, vp, gp = diff_attn_qkvg(hp, pos_p, l1_w_in)
            qs, ks, vs, gs = diff_attn_qkvg(hs, pos_s, l1_w_in)
            op = diff_attn_output(diff_attn_prompt(qp, kp, vp, lam), gp, l1_subln_g, l1_w_out)
            os_ = diff_attn_output(diff_attn_sample(qs, ks, vs, cache_k, cache_v, page_table, lam),
                                   gs, l1_subln_g, l1_w_out)
            xp = xp + gtp * op
            xs = xs + gts * os_
    y_prompt = rms_norm(xp, final_norm_g)
    y_sample = rms_norm(xs, final_norm_g)
    return (y_prompt, y_sample, lru_h_p, lru_h_s, conv_a_p, conv_a_s, conv_b_p, conv_b_s, kp, vp, ks, vs)
```

```python
import functools
import math

import jax
import jax.numpy as jnp
from jax import lax
from jax.experimental import pallas as pl
from jax.experimental.pallas import tpu as pltpu

F32 = jnp.float32
BF16 = jnp.bfloat16

D = 1024
LRU_HEADS = 16
LRU_BLK = D // LRU_HEADS
CONV_A = 4
CONV_B = 3
RG_C = 8.0
N_HEADS = 8
HEAD_DIM = 64
V_DIM = 128
ROT_DIM = 16
ROPE_THETA = 500000.0
LAMBDA_INIT = 0.8 - 0.6 * math.exp(-0.3 * 1)
SCALE = HEAD_DIM ** -0.5
PAGE_SIZE = 128
EPS = 1e-6
NEG = -1e30

N_SLICE = 4
SLICE_W = D // N_SLICE
CARRY_PAD = 8
VMEM_LIMIT = 56 * 1024 * 1024


def _silu(x):
    return x * jax.nn.sigmoid(x)


def _softplus(x):
    return jnp.maximum(x, 0.0) + jnp.log1p(jnp.exp(-jnp.abs(x)))


def _rms(x, g):
    return x * lax.rsqrt(jnp.mean(x * x, axis=-1, keepdims=True) + EPS) * g


def _dot(a, b):
    return jnp.dot(a, b, preferred_element_type=F32)


def _ada_kernel(c_ref, w_ref, b_ref, o_ref):
    s = _silu(c_ref[...]).astype(BF16)
    o_ref[...] = _dot(s, w_ref[...].astype(BF16)) + b_ref[...]


def _ada(c, w, b):
    n, tn = w.shape[1], 768
    return pl.pallas_call(
        _ada_kernel,
        grid=(n // tn,),
        in_specs=[pl.BlockSpec(c.shape, lambda j: (0, 0)),
                  pl.BlockSpec((D, tn), lambda j: (0, j)),
                  pl.BlockSpec((1, tn), lambda j: (0, j))],
        out_specs=pl.BlockSpec((c.shape[0], tn), lambda j: (0, j)),
        out_shape=jax.ShapeDtypeStruct((c.shape[0], n), F32),
        name="ada_mod",
    )(c, w, b.reshape(1, n))


def _lru_gates(xc, ri, br, bi, lam):
    w = xc.shape[1]
    r = jax.nn.sigmoid(ri[:, :w] + br)
    ig = jax.nn.sigmoid(ri[:, w:] + bi)
    log_a = (-RG_C * _softplus(-lam)) * r
    a = jnp.exp(log_a)
    u = jnp.sqrt(1.0 - jnp.exp(2.0 * log_a)) * (ig * xc)
    return a, u


def _l0_prompt_kernel(x_ref, mod_ref, ng_ref, win_ref, caw_ref, cab_ref, wg_ref, br_ref, bi_ref,
                      lam_ref, cbw_ref, wout_ref,
                      x1_ref, hl_ref, ca_ref, cb_ref,
                      hbf_s, exta_s, extb_s, xc_s, y_s, h_s, *, nb, tb):
    i = pl.program_id(0)
    pa = CARRY_PAD - (CONV_A - 1)
    pb = CARRY_PAD - (CONV_B - 1)

    @pl.when(i == 0)
    def _():
        exta_s[:, :, 0:CARRY_PAD, :] = jnp.zeros((N_SLICE, nb, CARRY_PAD, SLICE_W), F32)
        extb_s[:, :, 0:CARRY_PAD, :] = jnp.zeros((N_SLICE, nb, CARRY_PAD, SLICE_W), F32)
        h_s[...] = jnp.zeros(h_s.shape, F32)

    for b in range(nb):
        hn = _rms(x_ref[b], ng_ref[...])
        hn = hn * (1.0 + mod_ref[b:b + 1, D:2 * D]) + mod_ref[b:b + 1, 0:D]
        hbf_s[b * tb:(b + 1) * tb, :] = hn.astype(BF16)

    row8 = lax.broadcasted_iota(jnp.int32, (tb, SLICE_W), 0) & 7

    def slice_body(s, carry):
        hb = hbf_s[...]
        xa = _dot(hb, win_ref[0, s])
        caw = caw_ref[s]
        for b in range(nb):
            exta_s[s, b, CARRY_PAD:CARRY_PAD + tb, :] = xa[b * tb:(b + 1) * tb]
            acc = caw[0:1] * exta_s[s, b, pl.ds(pa, tb), :]
            for j in range(1, CONV_A):
                acc = acc + caw[j:j + 1] * exta_s[s, b, pl.ds(pa + j, tb), :]
            xc_s[b * tb:(b + 1) * tb, :] = acc + cab_ref[s]
            tail = exta_s[s, b, pl.ds(pa + tb, CONV_A - 1), :]
            exta_s[s, b, pa:CARRY_PAD, :] = tail
            ca_ref[s, b] = tail
        xc = xc_s[...]
        ri = _dot(xc.astype(BF16), wg_ref[s])
        a, u = _lru_gates(xc, ri, br_ref[s], bi_ref[s], lam_ref[s])
        ga = _dot(hb, win_ref[1, s])
        for b in range(nb):
            ab = a[b * tb:(b + 1) * tb]
            ub = u[b * tb:(b + 1) * tb]
            for d in (1, 2, 4):
                keep = row8 >= d
                a_sh = jnp.where(keep, pltpu.roll(ab, d, axis=0), 1.0)
                u_sh = jnp.where(keep, pltpu.roll(ub, d, axis=0), 0.0)
                ub = ab * u_sh + ub
                ab = ab * a_sh
            hprev = h_s[s, b:b + 1, :]
            hs = []
            for g in range(tb // 8):
                hg = ub[8 * g:8 * g + 8] + ab[8 * g:8 * g + 8] * hprev
                hprev = hg[7:8]
                hs.append(hg)
            h_s[s, b:b + 1, :] = hprev
            hl_ref[s, b:b + 1, :] = hprev
            ya = jnp.concatenate(hs, axis=0) * _silu(ga[b * tb:(b + 1) * tb])
            y_s[s, b * tb:(b + 1) * tb, :] = ya.astype(BF16)
        gate_b = _dot(hb, win_ref[2, s])
        u2 = _dot(hb, win_ref[3, s]) * _dot(hb, win_ref[4, s])
        gsil = _silu(_dot(hb, win_ref[5, s]))
        cbw = cbw_ref[s]
        for b in range(nb):
            extb_s[s, b, CARRY_PAD:CARRY_PAD + tb, :] = u2[b * tb:(b + 1) * tb]
            uc = cbw[0:1] * extb_s[s, b, pl.ds(pb, tb), :]
            for j in range(1, CONV_B):
                uc = uc + cbw[j:j + 1] * extb_s[s, b, pl.ds(pb + j, tb), :]
            tail = extb_s[s, b, pl.ds(pb + tb, CONV_B - 1), :]
            extb_s[s, b, pb:CARRY_PAD, :] = tail
            cb_ref[s, b] = tail
            yb = gate_b[b * tb:(b + 1) * tb] * uc * gsil[b * tb:(b + 1) * tb]
            y_s[N_SLICE + s, b * tb:(b + 1) * tb, :] = yb.astype(BF16)
        return carry

    lax.fori_loop(0, N_SLICE, slice_body, 0)

    out = _dot(y_s[0], wout_ref[0])
    for k in range(1, 2 * N_SLICE):
        out = out + _dot(y_s[k], wout_ref[k])
    for b in range(nb):
        x1_ref[b] = x_ref[b] + mod_ref[b:b + 1, 2 * D:3 * D] * out[b * tb:(b + 1) * tb]


def _const_spec(shape):
    nd = len(shape)
    return pl.BlockSpec(shape, lambda *_: (0,) * nd, pipeline_mode=pl.Buffered(1))


def _l0_prompt(x, mod, ng, w):
    nb, t, _ = x.shape
    tb = 64
    m = nb * tb
    kern = functools.partial(_l0_prompt_kernel, nb=nb, tb=tb)
    wshapes = [a.shape for a in w]
    return pl.pallas_call(
        kern,
        grid=(t // tb,),
        in_specs=[pl.BlockSpec((nb, tb, D), lambda i: (0, i, 0)),
                  _const_spec(mod.shape), _const_spec(ng.shape)] + [_const_spec(s) for s in wshapes],
        out_specs=[pl.BlockSpec((nb, tb, D), lambda i: (0, i, 0)),
                   pl.BlockSpec((N_SLICE, nb, SLICE_W), lambda i: (0, 0, 0)),
                   pl.BlockSpec((N_SLICE, nb, CONV_A - 1, SLICE_W), lambda i: (0, 0, 0, 0)),
                   pl.BlockSpec((N_SLICE, nb, CONV_B - 1, SLICE_W), lambda i: (0, 0, 0, 0))],
        out_shape=[jax.ShapeDtypeStruct((nb, t, D), F32),
                   jax.ShapeDtypeStruct((N_SLICE, nb, SLICE_W), F32),
                   jax.ShapeDtypeStruct((N_SLICE, nb, CONV_A - 1, SLICE_W), F32),
                   jax.ShapeDtypeStruct((N_SLICE, nb, CONV_B - 1, SLICE_W), F32)],
        scratch_shapes=[pltpu.VMEM((m, D), BF16),
                        pltpu.VMEM((N_SLICE, nb, CARRY_PAD + tb, SLICE_W), F32),
                        pltpu.VMEM((N_SLICE, nb, CARRY_PAD + tb, SLICE_W), F32),
                        pltpu.VMEM((m, SLICE_W), F32),
                        pltpu.VMEM((2 * N_SLICE, m, SLICE_W), BF16),
                        pltpu.VMEM((N_SLICE, nb, SLICE_W), F32)],
        compiler_params=pltpu.CompilerParams(dimension_semantics=("arbitrary",),
                                             vmem_limit_bytes=VMEM_LIMIT),
        name="l0_prompt",
    )(x, mod, ng, *w)


def _l0_sample_kernel(x_ref, mod_ref, ng_ref, h0_ref, ca0_ref, cb0_ref, win_ref, caw_ref, cab_ref,
                      wg_ref, br_ref, bi_ref, lam_ref, cbw_ref, wout_ref,
                      x1_ref, hl_ref, ca_ref, cb_ref):
    x = x_ref[...]
    hn = _rms(x, ng_ref[...])
    hb = (hn * (1.0 + mod_ref[:, D:2 * D]) + mod_ref[:, 0:D]).astype(BF16)
    out = jnp.zeros(x.shape, F32)
    for s in range(N_SLICE):
        sl = slice(s * SLICE_W, (s + 1) * SLICE_W)
        xa = _dot(hb, win_ref[0, s])
        caw = caw_ref[s]
        xc = caw[CONV_A - 1:CONV_A] * xa + cab_ref[s]
        for j in range(CONV_A - 1):
            xc = xc + caw[j:j + 1] * ca0_ref[j, :, sl]
        for j in range(CONV_A - 2):
            ca_ref[j, :, sl] = ca0_ref[j + 1, :, sl]
        ca_ref[CONV_A - 2, :, sl] = xa
        ri = _dot(xc.astype(BF16), wg_ref[s])
        a, u = _lru_gates(xc, ri, br_ref[s], bi_ref[s], lam_ref[s])
        h = a * h0_ref[:, sl] + u
        hl_ref[:, sl] = h
        ya = h * _silu(_dot(hb, win_ref[1, s]))
        gate_b = _dot(hb, win_ref[2, s])
        u2 = _dot(hb, win_ref[3, s]) * _dot(hb, win_ref[4, s])
        gsil = _silu(_dot(hb, win_ref[5, s]))
        cbw = cbw_ref[s]
        uc = cbw[CONV_B - 1:CONV_B] * u2
        for j in range(CONV_B - 1):
            uc = uc + cbw[j:j + 1] * cb0_ref[j, :, sl]
        for j in range(CONV_B - 2):
            cb_ref[j, :, sl] = cb0_ref[j + 1, :, sl]
        cb_ref[CONV_B - 2, :, sl] = u2
        yb = gate_b * uc * gsil
        out = out + _dot(ya.astype(BF16), wout_ref[s]) + _dot(yb.astype(BF16), wout_ref[N_SLICE + s])
    x1_ref[...] = x + mod_ref[:, 2 * D:3 * D] * out


def _l0_sample(x, mod, ng, h0, ca0, cb0, w):
    n = x.shape[0]
    return pl.pallas_call(
        _l0_sample_kernel,
        out_shape=[jax.ShapeDtypeStruct((n, D), F32), jax.ShapeDtypeStruct((n, D), F32),
                   jax.ShapeDtypeStruct((CONV_A - 1, n, D), F32),
                   jax.ShapeDtypeStruct((CONV_B - 1, n, D), F32)],
        compiler_params=pltpu.CompilerParams(vmem_limit_bytes=VMEM_LIMIT),
        name="l0_sample",
    )(x, mod, ng, h0, ca0, cb0, *w)


def _rope_lanes(x, tab_ref):
    outs = []
    for j in range(D // 128):
        xj = x[:, 128 * j:128 * (j + 1)]
        outs.append(xj * tab_ref[0] + pltpu.roll(xj, 8, axis=1) * tab_ref[1]
                    + pltpu.roll(xj, 120, axis=1) * tab_ref[2])
    return outs


def _qkvg_prompt_kernel(x_ref, mod_ref, ng_ref, wq_ref, wkt_ref, wv_ref, wg_ref, tq_ref, ck_ref, sk_ref,
                        q_ref, kt_ref, ktb_ref, v_ref, vb_ref, sg_ref, *, tb, tk):
    b = pl.program_id(0)
    hn = _rms(x_ref[0], ng_ref[...])
    hb = (hn * (1.0 + mod_ref[pl.ds(b, 1), D:2 * D]) + mod_ref[pl.ds(b, 1), 0:D]).astype(BF16)
    q = _dot(hb, wq_ref[...])
    for j, qj in enumerate(_rope_lanes(q, tq_ref)):
        q_ref[0, :, 128 * j:128 * (j + 1)] = qj.astype(BF16)
    kt = lax.dot_general(wkt_ref[...], hb, (((1,), (1,)), ((), ())), preferred_element_type=F32)
    k3 = kt.reshape(2 * N_HEADS, HEAD_DIM, tb)
    half = ROT_DIM // 2
    x1 = k3[:, 0:half, :]
    x2 = k3[:, half:ROT_DIM, :]
    ck = ck_ref[...]
    sk = sk_ref[...]
    n1 = x1 * ck - x2 * sk
    n2 = x2 * ck + x1 * sk
    kr = jnp.concatenate([n1, n2, k3[:, ROT_DIM:, :]], axis=1)
    kt_ref[0] = kr
    krb = kr.reshape(D, tb).astype(BF16)
    for j in range(tb // tk):
        ktb_ref[0, j] = krb[:, j * tk:(j + 1) * tk]
    v = _dot(hb, wv_ref[...])
    v_ref[0] = v
    vb_ref[0] = v.astype(BF16)
    sg_ref[0] = _silu(_dot(hb, wg_ref[...])).astype(BF16)


def _qkvg_prompt(x1, mod, ng, wq, wkt, wv, wg, tabq, ck, sk, *, tk):
    nb, t, _ = x1.shape
    tb = 512
    kern = functools.partial(_qkvg_prompt_kernel, tb=tb, tk=tk)
    row = pl.BlockSpec((1, tb, D), lambda b, i: (b, i, 0))
    return pl.pallas_call(
        kern,
        grid=(nb, t // tb),
        in_specs=[row, _const_spec(mod.shape), _const_spec(ng.shape),
                  _const_spec(wq.shape), _const_spec(wkt.shape), _const_spec(wv.shape), _const_spec(wg.shape),
                  pl.BlockSpec((3, tb, 128), lambda b, i: (0, i, 0)),
                  pl.BlockSpec((ROT_DIM // 2, tb), lambda b, i: (0, i)),
                  pl.BlockSpec((ROT_DIM // 2, tb), lambda b, i: (0, i))],
        out_specs=[row,
                   pl.BlockSpec((1, 2 * N_HEADS, HEAD_DIM, tb), lambda b, i: (b, 0, 0, i)),
                   pl.BlockSpec((1, tb // tk, D, tk), lambda b, i: (b, i, 0, 0)),
                   row, row, row],
        out_shape=[jax.ShapeDtypeStruct((nb, t, D), BF16),
                   jax.ShapeDtypeStruct((nb, 2 * N_HEADS, HEAD_DIM, t), F32),
                   jax.ShapeDtypeStruct((nb, t // tk, D, tk), BF16),
                   jax.ShapeDtypeStruct((nb, t, D), F32),
                   jax.ShapeDtypeStruct((nb, t, D), BF16),
                   jax.ShapeDtypeStruct((nb, t, D), BF16)],
        compiler_params=pltpu.CompilerParams(dimension_semantics=("arbitrary", "arbitrary"),
                                             vmem_limit_bytes=VMEM_LIMIT),
        name="qkvg_prompt",
    )(x1, mod, ng, wq, wkt, wv, wg, tabq, ck, sk)


def _qkvg_sample_kernel(x_ref, mod_ref, ng_ref, wq_ref, wk_ref, wv_ref, wg_ref, tq_ref, tk_ref,
                        q_ref, k_ref, v_ref, sg_ref):
    hn = _rms(x_ref[...], ng_ref[...])
    hb = (hn * (1.0 + mod_ref[:, D:2 * D]) + mod_ref[:, 0:D]).astype(BF16)
    for j, qj in enumerate(_rope_lanes(_dot(hb, wq_ref[...]), tq_ref)):
        q_ref[:, 128 * j:128 * (j + 1)] = qj
    for j, kj in enumerate(_rope_lanes(_dot(hb, wk_ref[...]), tk_ref)):
        k_ref[:, 128 * j:128 * (j + 1)] = kj
    v_ref[...] = _dot(hb, wv_ref[...])
    sg_ref[...] = _silu(_dot(hb, wg_ref[...]))


def _qkvg_sample(x1, mod, ng, wq, wk, wv, wg, tabq, tabk):
    n = x1.shape[0]
    return pl.pallas_call(
        _qkvg_sample_kernel,
        out_shape=[jax.ShapeDtypeStruct((n, D), F32)] * 4,
        compiler_params=pltpu.CompilerParams(vmem_limit_bytes=VMEM_LIMIT),
        name="qkvg_sample",
    )(x1, mod, ng, wq, wk, wv, wg, tabq, tabk)


def _diff_lambda(lp):
    e1 = jnp.exp(jnp.sum(lp[0:1] * lp[1:2], axis=-1, keepdims=True))
    e2 = jnp.exp(jnp.sum(lp[2:3] * lp[3:4], axis=-1, keepdims=True))
    return e1 - e2 + LAMBDA_INIT


def _subln(o, sub):
    return _rms(o, sub) * (1.0 - LAMBDA_INIT)


def _attn_prompt_kernel(q_ref, kt_ref, v_ref, sg_ref, x1_ref, mod_ref, lamp_ref, sub_ref, wout_ref, fng_ref,
                        y_ref, o_s, *, tq, tk):
    b = pl.program_id(0)
    qi = pl.program_id(1)
    lam = _diff_lambda(lamp_ref[...])
    lane = lax.broadcasted_iota(jnp.int32, (tq, 2 * HEAD_DIM), 1)
    rowi = lax.broadcasted_iota(jnp.int32, (2 * tq, tk), 0) & (tq - 1)
    coli = lax.broadcasted_iota(jnp.int32, (2 * tq, tk), 1)
    causal = coli <= rowi
    for h in range(N_HEADS):
        hs = slice(V_DIM * h, V_DIM * (h + 1))
        qh = q_ref[0, :, hs]
        zero = jnp.zeros_like(qh)
        qq = jnp.concatenate([jnp.where(lane < HEAD_DIM, qh, zero),
                              jnp.where(lane >= HEAD_DIM, qh, zero)], axis=0)

        def step(j, carry, masked):
            m, l, acc = carry
            s = _dot(qq, kt_ref[0, j, hs, :])
            if masked:
                s = jnp.where(causal, s, NEG)
            m_new = jnp.maximum(m, jnp.max(s, axis=-1, keepdims=True))
            corr = jnp.exp(m - m_new)
            p = jnp.exp(s - m_new)
            l = l * corr + jnp.sum(p, axis=-1, keepdims=True)
            vv = v_ref[0, pl.ds(pl.multiple_of(j * tk, tk), tk), hs]
            acc = acc * corr + _dot(p.astype(BF16), vv)
            return m_new, l, acc

        init = (jnp.full((2 * tq, 1), NEG, F32), jnp.zeros((2 * tq, 1), F32),
                jnp.zeros((2 * tq, V_DIM), F32))
        carry = lax.fori_loop(0, qi, lambda j, c: step(j, c, False), init)
        m, l, acc = step(qi, carry, True)
        o12 = acc / l
        o = _subln(o12[:tq] - lam * o12[tq:], sub_ref[...])
        o_s[:, hs] = (o * sg_ref[0, :, hs].astype(F32)).astype(BF16)
    out = _dot(o_s[...], wout_ref[...])
    x2 = x1_ref[0] + mod_ref[pl.ds(b, 1), 2 * D:3 * D] * out
    y_ref[0] = _rms(x2, fng_ref[...])


def _attn_prompt(q, ktb, vb, sg, x1, mod, lamp, sub, wout, fng, *, tq):
    nb, t, _ = q.shape
    tk = ktb.shape[-1]
    kern = functools.partial(_attn_prompt_kernel, tq=tq, tk=tk)
    row = pl.BlockSpec((1, tq, D), lambda b, i: (b, i, 0))
    return pl.pallas_call(
        kern,
        grid=(nb, t // tq),
        in_specs=[row,
                  pl.BlockSpec((1, t // tk, D, tk), lambda b, i: (b, 0, 0, 0)),
                  pl.BlockSpec((1, t, D), lambda b, i: (b, 0, 0)),
                  row, row, _const_spec(mod.shape), _const_spec(lamp.shape), _const_spec(sub.shape),
                  _const_spec(wout.shape), _const_spec(fng.shape)],
        out_specs=row,
        out_shape=jax.ShapeDtypeStruct((nb, t, D), F32),
        scratch_shapes=[pltpu.VMEM((tq, D), BF16)],
        compiler_params=pltpu.CompilerParams(dimension_semantics=("arbitrary", "arbitrary"),
                                             vmem_limit_bytes=VMEM_LIMIT),
        name="attn_prompt",
    )(q, ktb, vb, sg, x1, mod, lamp, sub, wout, fng)


N_ACC = 4


def _attn_sample_kernel(pt_ref, qb_ref, q2_ref, kn_ref, vn_ref, lamp_ref, *refs, g):
    k_refs = refs[:g]
    v_refs = refs[g:2 * g]
    o_ref = refs[2 * g]
    m_s, l_s, acc_s = refs[2 * g + 1:]
    p = pl.program_id(1)

    @pl.when(p == 0)
    def _():
        for c in range(2):
            m_s[c] = jnp.sum(q2_ref[c] * kn_ref[c], axis=-1, keepdims=True)
            l_s[c] = jnp.ones((N_HEADS, 1), F32)
            acc_s[c] = vn_ref[...]

    probs = []
    corrs = []
    for c in range(2):
        qbc = qb_ref[c]
        s = jnp.concatenate([jnp.sum(k_refs[i][:, c] * qbc, axis=1) for i in range(g)], axis=1)
        m_old = m_s[c]
        m_new = jnp.maximum(m_old, jnp.max(s, axis=-1, keepdims=True))
        corr = jnp.exp(m_old - m_new)
        pr = jnp.exp(s - m_new)
        l_s[c] = l_s[c] * corr + jnp.sum(pr, axis=-1, keepdims=True)
        m_s[c] = m_new
        probs.append(pr)
        corrs.append(corr)
    accs = [[acc_s[c] * corrs[c]] + [jnp.zeros((N_HEADS, V_DIM), F32)] * (N_ACC - 1) for c in range(2)]
    for i in range(g):
        for pos in range(PAGE_SIZE):
            vrow = v_refs[i][pos]
            col = i * PAGE_SIZE + pos
            for c in range(2):
                a = pos % N_ACC
                accs[c][a] = accs[c][a] + probs[c][:, col:col + 1] * vrow
    for c in range(2):
        tot = accs[c][0]
        for a in range(1, N_ACC):
            tot = tot + accs[c][a]
        acc_s[c] = tot

    @pl.when(p == pl.num_programs(1) - 1)
    def _():
        lam = _diff_lambda(lamp_ref[...])
        o_ref[...] = acc_s[0] / l_s[0] - lam * (acc_s[1] / l_s[1])


def _attn_sample(page_table, qb, q2, kn, vn, lamp, ck5, cv, *, g):
    n, n_pages = page_table.shape
    kern = functools.partial(_attn_sample_kernel, g=g)

    def kmap(i):
        return lambda b, p, pt: (pt[b, p * g + i], 0, 0, 0, 0)

    def vmap_(i):
        return lambda b, p, pt: (pt[b, p * g + i], 0, 0, 0)

    k_specs = [pl.BlockSpec((None, N_HEADS, 2, HEAD_DIM, PAGE_SIZE), kmap(i)) for i in range(g)]
    v_specs = [pl.BlockSpec((None, PAGE_SIZE, N_HEADS, V_DIM), vmap_(i)) for i in range(g)]
    grid_spec = pltpu.PrefetchScalarGridSpec(
        num_scalar_prefetch=1,
        grid=(n, n_pages // g),
        in_specs=[pl.BlockSpec((None, 2, N_HEADS, HEAD_DIM, PAGE_SIZE), lambda b, p, pt: (b, 0, 0, 0, 0)),
                  pl.BlockSpec((None, 2, N_HEADS, HEAD_DIM), lambda b, p, pt: (b, 0, 0, 0)),
                  pl.BlockSpec((None, 2, N_HEADS, HEAD_DIM), lambda b, p, pt: (b, 0, 0, 0)),
                  pl.BlockSpec((None, N_HEADS, V_DIM), lambda b, p, pt: (b, 0, 0)),
                  pl.BlockSpec(lamp.shape, lambda b, p, pt: (0, 0))] + k_specs + v_specs,
        out_specs=pl.BlockSpec((None, N_HEADS, V_DIM), lambda b, p, pt: (b, 0, 0)),
        scratch_shapes=[pltpu.VMEM((2, N_HEADS, 1), F32), pltpu.VMEM((2, N_HEADS, 1), F32),
                        pltpu.VMEM((2, N_HEADS, V_DIM), F32)],
    )
    return pl.pallas_call(
        kern,
        grid_spec=grid_spec,
        out_shape=jax.ShapeDtypeStruct((n, N_HEADS, V_DIM), F32),
        compiler_params=pltpu.CompilerParams(dimension_semantics=("arbitrary", "arbitrary"),
                                             vmem_limit_bytes=VMEM_LIMIT),
        name="attn_sample",
    )(page_table, qb, q2, kn, vn, lamp, *([ck5] * g), *([cv] * g))


def _epi_sample_kernel(o_ref, sg_ref, x1_ref, mod_ref, sub_ref, wout_ref, fng_ref, y_ref):
    parts = []
    for h in range(N_HEADS):
        hs = slice(V_DIM * h, V_DIM * (h + 1))
        parts.append((_subln(o_ref[:, hs], sub_ref[...]) * sg_ref[:, hs]).astype(BF16))
    out = _dot(jnp.concatenate(parts, axis=1), wout_ref[...])
    x2 = x1_ref[...] + mod_ref[:, 2 * D:3 * D] * out
    y_ref[...] = _rms(x2, fng_ref[...])


def _epi_sample(o, sg, x1, mod, sub, wout, fng):
    return pl.pallas_call(
        _epi_sample_kernel,
        out_shape=jax.ShapeDtypeStruct(x1.shape, F32),
        name="epi_sample",
    )(o, sg, x1, mod, sub, wout, fng)


def _block_diag_slices(w):
    per = LRU_HEADS // N_SLICE
    w4 = w.reshape(N_SLICE, per, LRU_BLK, LRU_BLK)
    eye = jnp.eye(per, dtype=w.dtype)
    return jnp.einsum('skcd,kl->skcld', w4, eye).reshape(N_SLICE, SLICE_W, SLICE_W)


def _sliced(p):
    return p.reshape(p.shape[0], N_SLICE, SLICE_W).transpose(1, 0, 2)


def _l0_weights(w_in, conv_a_w, conv_a_b, wr, br, wi, bi, lam, conv_b_w, w_out):
    win = w_in.astype(BF16).reshape(D, 6, N_SLICE, SLICE_W).transpose(1, 2, 0, 3)
    wg = jnp.concatenate([_block_diag_slices(wr), _block_diag_slices(wi)], axis=-1).astype(BF16)
    wout = w_out.astype(BF16).reshape(2 * N_SLICE, SLICE_W, D)
    return (win, _sliced(conv_a_w), _sliced(conv_a_b[None]), wg, _sliced(br[None]), _sliced(bi[None]),
            _sliced(lam[None]), _sliced(conv_b_w), wout)


def _rope_lane_tables(pos, scale):
    half = ROT_DIM // 2
    inv = jnp.power(jnp.float32(ROPE_THETA), -jnp.arange(0, ROT_DIM, 2, dtype=F32) / ROT_DIM)
    ang = pos.astype(F32)[:, None] * inv[None, :]
    lane = jnp.arange(128)
    within = lane % HEAD_DIM
    cos = jnp.cos(ang)[:, lane % half]
    sin = jnp.sin(ang)[:, lane % half]
    c = jnp.where(within < ROT_DIM, cos, 1.0)
    sa = jnp.where((within >= half) & (within < ROT_DIM), sin, 0.0)
    sb = jnp.where(within < half, -sin, 0.0)
    return jnp.stack([c, sa, sb]) * scale


def kernel(x_prompt, x_sample, state_lru_h, state_conv_a, state_conv_b, cache_k, cache_v, page_table,
           c_prompt, c_sample, l0_norm_g, l0_ada_w, l0_ada_b, l0_w_in, l0_conv_a_w, l0_conv_a_b,
           l0_lru_wr, l0_lru_br, l0_lru_wi, l0_lru_bi, l0_lru_lam, l0_conv_b_w, l0_w_out,
           l1_norm_g, l1_ada_w, l1_ada_b, l1_w_in, l1_lam_q1, l1_lam_k1, l1_lam_q2, l1_lam_k2,
           l1_subln_g, l1_w_out, final_norm_g):
    nb, t, _ = x_prompt.shape
    ns = x_sample.shape[0]
    n_pages = page_table.shape[1]
    past_len = n_pages * PAGE_SIZE
    tq = 256

    c_all = jnp.concatenate([c_prompt, c_sample], axis=0)
    mod0 = _ada(c_all, l0_ada_w, l0_ada_b)
    mod1 = _ada(c_all, l1_ada_w, l1_ada_b)

    w0 = _l0_weights(l0_w_in, l0_conv_a_w, l0_conv_a_b, l0_lru_wr, l0_lru_br, l0_lru_wi, l0_lru_bi,
                     l0_lru_lam, l0_conv_b_w, l0_w_out)
    ng0 = l0_norm_g.reshape(1, D)
    x1p, hl_p, ca_p, cb_p = _l0_prompt(x_prompt, mod0[:nb], ng0, w0)
    x1s, hl_s, ca_s, cb_s = _l0_sample(x_sample.reshape(ns, D), mod0[nb:], ng0, state_lru_h,
                                       state_conv_a.transpose(1, 0, 2), state_conv_b.transpose(1, 0, 2), w0)
    lru_h_p = hl_p.transpose(1, 0, 2).reshape(nb, D)
    conv_a_p = ca_p.transpose(1, 2, 0, 3).reshape(nb, CONV_A - 1, D)
    conv_b_p = cb_p.transpose(1, 2, 0, 3).reshape(nb, CONV_B - 1, D)
    conv_a_s = ca_s.transpose(1, 0, 2)
    conv_b_s = cb_s.transpose(1, 0, 2)

    w1 = l1_w_in.astype(BF16)
    wq, wk, wv, wg = w1[:, 0:D], w1[:, D:2 * D], w1[:, 2 * D:3 * D], w1[:, 3 * D:4 * D]
    ng1 = l1_norm_g.reshape(1, D)
    pos_p = jnp.arange(t, dtype=jnp.int32)
    pos_s = jnp.full((1,), past_len, jnp.int32)
    half = ROT_DIM // 2
    inv = jnp.power(jnp.float32(ROPE_THETA), -jnp.arange(0, ROT_DIM, 2, dtype=F32) / ROT_DIM)
    ang_t = inv[:, None] * pos_p.astype(F32)[None, :]
    q_p, kt_p, ktb_p, v_p, vb_p, sg_p = _qkvg_prompt(
        x1p, mod1[:nb], ng1, wq, wk.T, wv, wg, _rope_lane_tables(pos_p, SCALE),
        jnp.cos(ang_t), jnp.sin(ang_t), tk=tq)
    q_s, k_s, v_s, sg_s = _qkvg_sample(x1s, mod1[nb:], ng1, wq, wk, wv, wg,
                                       _rope_lane_tables(pos_s, SCALE), _rope_lane_tables(pos_s, 1.0))

    lamp = jnp.stack([l1_lam_q1, l1_lam_k1, l1_lam_q2, l1_lam_k2])
    sub = l1_subln_g.reshape(1, V_DIM)
    wout1 = l1_w_out.astype(BF16)
    fng = final_norm_g.reshape(1, D)
    y_p = _attn_prompt(q_p, ktb_p, vb_p, sg_p, x1p, mod1[:nb], lamp, sub, wout1, fng, tq=tq)

    q2 = q_s.reshape(ns, N_HEADS, 2, HEAD_DIM).transpose(0, 2, 1, 3)
    kn = k_s.reshape(ns, N_HEADS, 2, HEAD_DIM).transpose(0, 2, 1, 3)
    qb = jnp.broadcast_to(q2[..., None], (ns, 2, N_HEADS, HEAD_DIM, PAGE_SIZE))
    ck5 = cache_k.transpose(0, 2, 3, 1).reshape(cache_k.shape[0], N_HEADS, 2, HEAD_DIM, PAGE_SIZE)
    o_s = _attn_sample(page_table, qb, q2, kn, v_s.reshape(ns, N_HEADS, V_DIM), lamp, ck5, cache_v, g=8)
    y_s = _epi_sample(o_s.reshape(ns, D), sg_s, x1s, mod1[nb:], sub, wout1, fng)

    k_p = kt_p.transpose(0, 3, 1, 2)
    return (y_p, y_s.reshape(ns, 1, D), lru_h_p, hl_s, conv_a_p, conv_a_s, conv_b_p, conv_b_s,
            k_p, v_p.reshape(nb, t, N_HEADS, V_DIM), k_s.reshape(ns, 1, 2 * N_HEADS, HEAD_DIM),
            v_s.reshape(ns, 1, N_HEADS, V_DIM))
```

```python
import functools
import math

import jax
import jax.numpy as jnp
from jax import lax
from jax.experimental import pallas as pl
from jax.experimental.pallas import tpu as pltpu

F32 = jnp.float32
BF16 = jnp.bfloat16

D = 1024
LRU_HEADS = 16
LRU_BLK = D // LRU_HEADS
CONV_A = 4
CONV_B = 3
RG_C = 8.0
N_HEADS = 8
HEAD_DIM = 64
V_DIM = 128
ROT_DIM = 16
ROPE_THETA = 500000.0
LAMBDA_INIT = 0.8 - 0.6 * math.exp(-0.3 * 1)
SCALE = HEAD_DIM ** -0.5
LOG2E = math.log2(math.e)
PAGE_SIZE = 128
EPS = 1e-6
NEG = -1e30

N_SLICE = 4
SLICE_W = D // N_SLICE
CARRY_PAD = 8
VMEM_LIMIT = 56 * 1024 * 1024


def _silu(x):
    return x * jax.nn.sigmoid(x)


def _softplus(x):
    return jnp.maximum(x, 0.0) + jnp.log1p(jnp.exp(-jnp.abs(x)))


def _rms(x, g):
    return x * lax.rsqrt(jnp.mean(x * x, axis=-1, keepdims=True) + EPS) * g


def _dot(a, b):
    return jnp.dot(a, b, preferred_element_type=F32)


def _ada_kernel(c_ref, w_ref, b_ref, o_ref):
    s = _silu(c_ref[...]).astype(BF16)
    o_ref[...] = _dot(s, w_ref[...].astype(BF16)) + b_ref[...]


def _ada(c, w, b):
    n, tn = w.shape[1], 768
    return pl.pallas_call(
        _ada_kernel,
        grid=(n // tn,),
        in_specs=[pl.BlockSpec(c.shape, lambda j: (0, 0)),
                  pl.BlockSpec((D, tn), lambda j: (0, j)),
                  pl.BlockSpec((1, tn), lambda j: (0, j))],
        out_specs=pl.BlockSpec((c.shape[0], tn), lambda j: (0, j)),
        out_shape=jax.ShapeDtypeStruct((c.shape[0], n), F32),
        name="ada_mod",
    )(c, w, b.reshape(1, n))


def _lru_gates(xc, ri, br, bi, lam):
    w = xc.shape[1]
    r = jax.nn.sigmoid(ri[:, :w] + br)
    ig = jax.nn.sigmoid(ri[:, w:] + bi)
    log_a = (-RG_C * _softplus(-lam)) * r
    a = jnp.exp(log_a)
    u = jnp.sqrt(1.0 - jnp.exp(2.0 * log_a)) * (ig * xc)
    return a, u


def _l0_prompt_kernel(x_ref, mod_ref, ng_ref, win_ref, caw_ref, cab_ref, wg_ref, br_ref, bi_ref,
                      lam_ref, cbw_ref, wout_ref,
                      x1_ref, hl_ref, ca_ref, cb_ref,
                      hbf_s, exta_s, extb_s, xc_s, y_s, h_s, *, nb, tb):
    i = pl.program_id(0)
    pa = CARRY_PAD - (CONV_A - 1)
    pb = CARRY_PAD - (CONV_B - 1)

    @pl.when(i == 0)
    def _():
        exta_s[:, :, 0:CARRY_PAD, :] = jnp.zeros((N_SLICE, nb, CARRY_PAD, SLICE_W), F32)
        extb_s[:, :, 0:CARRY_PAD, :] = jnp.zeros((N_SLICE, nb, CARRY_PAD, SLICE_W), F32)
        h_s[...] = jnp.zeros(h_s.shape, F32)

    for b in range(nb):
        hn = _rms(x_ref[b], ng_ref[...])
        hn = hn * (1.0 + mod_ref[b:b + 1, D:2 * D]) + mod_ref[b:b + 1, 0:D]
        hbf_s[b * tb:(b + 1) * tb, :] = hn.astype(BF16)

    row8 = lax.broadcasted_iota(jnp.int32, (tb, SLICE_W), 0) & 7

    def slice_body(s, carry):
        hb = hbf_s[...]
        xa = _dot(hb, win_ref[0, s])
        caw = caw_ref[s]
        for b in range(nb):
            exta_s[s, b, CARRY_PAD:CARRY_PAD + tb, :] = xa[b * tb:(b + 1) * tb]
            acc = caw[0:1] * exta_s[s, b, pl.ds(pa, tb), :]
            for j in range(1, CONV_A):
                acc = acc + caw[j:j + 1] * exta_s[s, b, pl.ds(pa + j, tb), :]
            xc_s[b * tb:(b + 1) * tb, :] = acc + cab_ref[s]
            tail = exta_s[s, b, pl.ds(pa + tb, CONV_A - 1), :]
            exta_s[s, b, pa:CARRY_PAD, :] = tail
            ca_ref[s, b] = tail
        xc = xc_s[...]
        ri = _dot(xc.astype(BF16), wg_ref[s])
        a, u = _lru_gates(xc, ri, br_ref[s], bi_ref[s], lam_ref[s])
        ga = _dot(hb, win_ref[1, s])
        for b in range(nb):
            ab = a[b * tb:(b + 1) * tb]
            ub = u[b * tb:(b + 1) * tb]
            for d in (1, 2, 4):
                keep = row8 >= d
                a_sh = jnp.where(keep, pltpu.roll(ab, d, axis=0), 1.0)
                u_sh = jnp.where(keep, pltpu.roll(ub, d, axis=0), 0.0)
                ub = ab * u_sh + ub
                ab = ab * a_sh
            hprev = h_s[s, b:b + 1, :]
            hs = []
            for g in range(tb // 8):
                hg = ub[8 * g:8 * g + 8] + ab[8 * g:8 * g + 8] * hprev
                hprev = hg[7:8]
                hs.append(hg)
            h_s[s, b:b + 1, :] = hprev
            hl_ref[s, b:b + 1, :] = hprev
            ya = jnp.concatenate(hs, axis=0) * _silu(ga[b * tb:(b + 1) * tb])
            y_s[s, b * tb:(b + 1) * tb, :] = ya.astype(BF16)
        gate_b = _dot(hb, win_ref[2, s])
        u2 = _dot(hb, win_ref[3, s]) * _dot(hb, win_ref[4, s])
        gsil = _silu(_dot(hb, win_ref[5, s]))
        cbw = cbw_ref[s]
        for b in range(nb):
            extb_s[s, b, CARRY_PAD:CARRY_PAD + tb, :] = u2[b * tb:(b + 1) * tb]
            uc = cbw[0:1] * extb_s[s, b, pl.ds(pb, tb), :]
            for j in range(1, CONV_B):
                uc = uc + cbw[j:j + 1] * extb_s[s, b, pl.ds(pb + j, tb), :]
            tail = extb_s[s, b, pl.ds(pb + tb, CONV_B - 1), :]
            extb_s[s, b, pb:CARRY_PAD, :] = tail
            cb_ref[s, b] = tail
            yb = gate_b[b * tb:(b + 1) * tb] * uc * gsil[b * tb:(b + 1) * tb]
            y_s[N_SLICE + s, b * tb:(b + 1) * tb, :] = yb.astype(BF16)
        return carry

    lax.fori_loop(0, N_SLICE, slice_body, 0)

    out = _dot(y_s[0], wout_ref[0])
    for k in range(1, 2 * N_SLICE):
        out = out + _dot(y_s[k], wout_ref[k])
    for b in range(nb):
        x1_ref[b] = x_ref[b] + mod_ref[b:b + 1, 2 * D:3 * D] * out[b * tb:(b + 1) * tb]


def _const_spec(shape):
    nd = len(shape)
    return pl.BlockSpec(shape, lambda *_: (0,) * nd, pipeline_mode=pl.Buffered(1))


def _l0_prompt(x, mod, ng, w):
    nb, t, _ = x.shape
    tb = 64
    m = nb * tb
    kern = functools.partial(_l0_prompt_kernel, nb=nb, tb=tb)
    wshapes = [a.shape for a in w]
    return pl.pallas_call(
        kern,
        grid=(t // tb,),
        in_specs=[pl.BlockSpec((nb, tb, D), lambda i: (0, i, 0)),
                  _const_spec(mod.shape), _const_spec(ng.shape)] + [_const_spec(s) for s in wshapes],
        out_specs=[pl.BlockSpec((nb, tb, D), lambda i: (0, i, 0)),
                   pl.BlockSpec((N_SLICE, nb, SLICE_W), lambda i: (0, 0, 0)),
                   pl.BlockSpec((N_SLICE, nb, CONV_A - 1, SLICE_W), lambda i: (0, 0, 0, 0)),
                   pl.BlockSpec((N_SLICE, nb, CONV_B - 1, SLICE_W), lambda i: (0, 0, 0, 0))],
        out_shape=[jax.ShapeDtypeStruct((nb, t, D), F32),
                   jax.ShapeDtypeStruct((N_SLICE, nb, SLICE_W), F32),
                   jax.ShapeDtypeStruct((N_SLICE, nb, CONV_A - 1, SLICE_W), F32),
                   jax.ShapeDtypeStruct((N_SLICE, nb, CONV_B - 1, SLICE_W), F32)],
        scratch_shapes=[pltpu.VMEM((m, D), BF16),
                        pltpu.VMEM((N_SLICE, nb, CARRY_PAD + tb, SLICE_W), F32),
                        pltpu.VMEM((N_SLICE, nb, CARRY_PAD + tb, SLICE_W), F32),
                        pltpu.VMEM((m, SLICE_W), F32),
                        pltpu.VMEM((2 * N_SLICE, m, SLICE_W), BF16),
                        pltpu.VMEM((N_SLICE, nb, SLICE_W), F32)],
        compiler_params=pltpu.CompilerParams(dimension_semantics=("arbitrary",),
                                             vmem_limit_bytes=VMEM_LIMIT),
        name="l0_prompt",
    )(x, mod, ng, *w)


def _l0_sample_kernel(x_ref, mod_ref, ng_ref, h0_ref, ca0_ref, cb0_ref, win_ref, caw_ref, cab_ref,
                      wg_ref, br_ref, bi_ref, lam_ref, cbw_ref, wout_ref,
                      x1_ref, hl_ref, ca_ref, cb_ref):
    x = x_ref[...]
    hn = _rms(x, ng_ref[...])
    hb = (hn * (1.0 + mod_ref[:, D:2 * D]) + mod_ref[:, 0:D]).astype(BF16)
    out = jnp.zeros(x.shape, F32)
    for s in range(N_SLICE):
        sl = slice(s * SLICE_W, (s + 1) * SLICE_W)
        xa = _dot(hb, win_ref[0, s])
        caw = caw_ref[s]
        xc = caw[CONV_A - 1:CONV_A] * xa + cab_ref[s]
        for j in range(CONV_A - 1):
            xc = xc + caw[j:j + 1] * ca0_ref[j, :, sl]
        for j in range(CONV_A - 2):
            ca_ref[j, :, sl] = ca0_ref[j + 1, :, sl]
        ca_ref[CONV_A - 2, :, sl] = xa
        ri = _dot(xc.astype(BF16), wg_ref[s])
        a, u = _lru_gates(xc, ri, br_ref[s], bi_ref[s], lam_ref[s])
        h = a * h0_ref[:, sl] + u
        hl_ref[:, sl] = h
        ya = h * _silu(_dot(hb, win_ref[1, s]))
        gate_b = _dot(hb, win_ref[2, s])
        u2 = _dot(hb, win_ref[3, s]) * _dot(hb, win_ref[4, s])
        gsil = _silu(_dot(hb, win_ref[5, s]))
        cbw = cbw_ref[s]
        uc = cbw[CONV_B - 1:CONV_B] * u2
        for j in range(CONV_B - 1):
            uc = uc + cbw[j:j + 1] * cb0_ref[j, :, sl]
        for j in range(CONV_B - 2):
            cb_ref[j, :, sl] = cb0_ref[j + 1, :, sl]
        cb_ref[CONV_B - 2, :, sl] = u2
        yb = gate_b * uc * gsil
        out = out + _dot(ya.astype(BF16), wout_ref[s]) + _dot(yb.astype(BF16), wout_ref[N_SLICE + s])
    x1_ref[...] = x + mod_ref[:, 2 * D:3 * D] * out


def _l0_sample(x, mod, ng, h0, ca0, cb0, w):
    n = x.shape[0]
    return pl.pallas_call(
        _l0_sample_kernel,
        out_shape=[jax.ShapeDtypeStruct((n, D), F32), jax.ShapeDtypeStruct((n, D), F32),
                   jax.ShapeDtypeStruct((CONV_A - 1, n, D), F32),
                   jax.ShapeDtypeStruct((CONV_B - 1, n, D), F32)],
        compiler_params=pltpu.CompilerParams(vmem_limit_bytes=VMEM_LIMIT),
        name="l0_sample",
    )(x, mod, ng, h0, ca0, cb0, *w)


def _rope_lanes(x, tab_ref):
    outs = []
    for j in range(D // 128):
        xj = x[:, 128 * j:128 * (j + 1)]
        outs.append(xj * tab_ref[0] + pltpu.roll(xj, 8, axis=1) * tab_ref[1]
                    + pltpu.roll(xj, 120, axis=1) * tab_ref[2])
    return outs


def _rope_rows(xt, c, s, scale):
    x3 = xt.reshape(2 * N_HEADS, HEAD_DIM, xt.shape[1])
    half = ROT_DIM // 2
    x1 = x3[:, 0:half, :]
    x2 = x3[:, half:ROT_DIM, :]
    rest = x3[:, ROT_DIM:, :]
    if scale != 1.0:
        rest = rest * scale
    return jnp.concatenate([x1 * c - x2 * s, x2 * c + x1 * s, rest], axis=1)


_NT = (((1,), (1,)), ((), ()))


def _qkvg_prompt_kernel(x_ref, mod_ref, ng_ref, wqt_ref, wkt_ref, wk_ref, wv_ref, wvt_ref, wgt_ref,
                        tk_ref, ck_ref, sk_ref,
                        qt_ref, kt_ref, kb_ref, v_ref, vtb_ref, sgt_ref, *, tb, tk):
    b = pl.program_id(0)
    hn = _rms(x_ref[0], ng_ref[...])
    hb = (hn * (1.0 + mod_ref[pl.ds(b, 1), D:2 * D]) + mod_ref[pl.ds(b, 1), 0:D]).astype(BF16)
    ck = ck_ref[...]
    sk = sk_ref[...]
    qs = SCALE * LOG2E
    qt = lax.dot_general(wqt_ref[...], hb, _NT, preferred_element_type=F32)
    qt_ref[0] = _rope_rows(qt, ck * qs, sk * qs, qs).reshape(D, tb).astype(BF16)
    kt = lax.dot_general(wkt_ref[...], hb, _NT, preferred_element_type=F32)
    kt_ref[0] = _rope_rows(kt, ck, sk, 1.0)
    for j, kj in enumerate(_rope_lanes(_dot(hb, wk_ref[...]), tk_ref)):
        kb_ref[0, :, 128 * j:128 * (j + 1)] = kj.astype(BF16)
    v_ref[0] = _dot(hb, wv_ref[...])
    vt = lax.dot_general(wvt_ref[...], hb, _NT, preferred_element_type=F32).astype(BF16)
    for j in range(tb // tk):
        vtb_ref[0, j] = vt[:, j * tk:(j + 1) * tk]
    sgt_ref[0] = _silu(lax.dot_general(wgt_ref[...], hb, _NT, preferred_element_type=F32)).astype(BF16)


def _qkvg_prompt(x1, mod, ng, wqt, wkt, wk, wv, wvt, wgt, tabk, ck, sk, *, tk):
    nb, t, _ = x1.shape
    tb = 512
    kern = functools.partial(_qkvg_prompt_kernel, tb=tb, tk=tk)
    row = pl.BlockSpec((1, tb, D), lambda b, i: (b, i, 0))
    col = pl.BlockSpec((1, D, tb), lambda b, i: (b, 0, i))
    wspec = _const_spec(wqt.shape)
    return pl.pallas_call(
        kern,
        grid=(nb, t // tb),
        in_specs=[row, _const_spec(mod.shape), _const_spec(ng.shape)] + [wspec] * 6 + [
            pl.BlockSpec((3, tb, 128), lambda b, i: (0, i, 0)),
            pl.BlockSpec((ROT_DIM // 2, tb), lambda b, i: (0, i)),
            pl.BlockSpec((ROT_DIM // 2, tb), lambda b, i: (0, i))],
        out_specs=[col,
                   pl.BlockSpec((1, 2 * N_HEADS, HEAD_DIM, tb), lambda b, i: (b, 0, 0, i)),
                   row, row,
                   pl.BlockSpec((1, tb // tk, D, tk), lambda b, i: (b, i, 0, 0)),
                   col],
        out_shape=[jax.ShapeDtypeStruct((nb, D, t), BF16),
                   jax.ShapeDtypeStruct((nb, 2 * N_HEADS, HEAD_DIM, t), F32),
                   jax.ShapeDtypeStruct((nb, t, D), BF16),
                   jax.ShapeDtypeStruct((nb, t, D), F32),
                   jax.ShapeDtypeStruct((nb, t // tk, D, tk), BF16),
                   jax.ShapeDtypeStruct((nb, D, t), BF16)],
        compiler_params=pltpu.CompilerParams(dimension_semantics=("arbitrary", "arbitrary"),
                                             vmem_limit_bytes=VMEM_LIMIT),
        name="qkvg_prompt",
    )(x1, mod, ng, wqt, wkt, wk, wv, wvt, wgt, tabk, ck, sk)


def _qkvg_sample_kernel(x_ref, mod_ref, ng_ref, wq_ref, wk_ref, wv_ref, wg_ref, tq_ref, tk_ref,
                        q_ref, k_ref, v_ref, sg_ref):
    hn = _rms(x_ref[...], ng_ref[...])
    hb = (hn * (1.0 + mod_ref[:, D:2 * D]) + mod_ref[:, 0:D]).astype(BF16)
    for j, qj in enumerate(_rope_lanes(_dot(hb, wq_ref[...]), tq_ref)):
        q_ref[:, 128 * j:128 * (j + 1)] = qj
    for j, kj in enumerate(_rope_lanes(_dot(hb, wk_ref[...]), tk_ref)):
        k_ref[:, 128 * j:128 * (j + 1)] = kj
    v_ref[...] = _dot(hb, wv_ref[...])
    sg_ref[...] = _silu(_dot(hb, wg_ref[...]))


def _qkvg_sample(x1, mod, ng, wq, wk, wv, wg, tabq, tabk):
    n = x1.shape[0]
    return pl.pallas_call(
        _qkvg_sample_kernel,
        out_shape=[jax.ShapeDtypeStruct((n, D), F32)] * 4,
        compiler_params=pltpu.CompilerParams(vmem_limit_bytes=VMEM_LIMIT),
        name="qkvg_sample",
    )(x1, mod, ng, wq, wk, wv, wg, tabq, tabk)


def _diff_lambda(lp):
    e1 = jnp.exp(jnp.sum(lp[0:1] * lp[1:2], axis=-1, keepdims=True))
    e2 = jnp.exp(jnp.sum(lp[2:3] * lp[3:4], axis=-1, keepdims=True))
    return e1 - e2 + LAMBDA_INIT


def _subln(o, sub):
    return _rms(o, sub) * (1.0 - LAMBDA_INIT)


def _attn_prompt_kernel(qt_ref, kb_ref, vt_ref, sgt_ref, x1_ref, mod_ref, lamp_ref, sub_ref, wout_ref, fng_ref,
                        y_ref, o_s, qq_s, m_s, l_s, acc_s, st_s, *, tq, tk):
    b = pl.program_id(0)
    qi = pl.program_id(1)
    row = lax.broadcasted_iota(jnp.int32, (2 * HEAD_DIM, tq), 0)
    for h in range(N_HEADS):
        qh = qt_ref[0, V_DIM * h:V_DIM * (h + 1), :]
        zero = jnp.zeros_like(qh)
        qq_s[h, :, 0:tq] = jnp.where(row < HEAD_DIM, qh, zero)
        qq_s[h, :, tq:2 * tq] = jnp.where(row >= HEAD_DIM, qh, zero)
    m_s[...] = jnp.full(m_s.shape, NEG, F32)
    l_s[...] = jnp.zeros(l_s.shape, F32)
    acc_s[...] = jnp.zeros(acc_s.shape, F32)

    def scores(j, h):
        hs = slice(V_DIM * h, V_DIM * (h + 1))
        return _dot(kb_ref[0, pl.ds(pl.multiple_of(j * tk, tk), tk), hs], qq_s[h])

    def update(j, h, st, masked):
        if masked:
            keyi = lax.broadcasted_iota(jnp.int32, (tk, 2 * tq), 0)
            qryi = lax.broadcasted_iota(jnp.int32, (tk, 2 * tq), 1) & (tq - 1)
            st = jnp.where(keyi <= qryi, st, NEG)
        m_old = m_s[h]
        m_new = jnp.maximum(m_old, jnp.max(st, axis=0, keepdims=True))
        corr = jnp.exp2(m_old - m_new)
        p = jnp.exp2(st - m_new)
        m_s[h] = m_new
        l_s[h] = l_s[h] * corr + jnp.sum(p, axis=0, keepdims=True)
        acc_s[h] = acc_s[h] * corr + _dot(vt_ref[0, j, V_DIM * h:V_DIM * (h + 1), :], p.astype(BF16))

    for h in range(N_HEADS):
        st_s[h] = scores(0, h)

    def body(j, carry):
        for h in range(N_HEADS):
            cur = st_s[h]
            nxt = scores(j + 1, h)
            update(j, h, cur, False)
            st_s[h] = nxt
        return carry

    lax.fori_loop(0, qi, body, 0)
    for h in range(N_HEADS):
        update(qi, h, st_s[h], True)

    lam = _diff_lambda(lamp_ref[...])
    for h in range(N_HEADS):
        hs = slice(V_DIM * h, V_DIM * (h + 1))
        o12 = acc_s[h] / l_s[h]
        ot = o12[:, :tq] - lam * o12[:, tq:]
        ms = jnp.mean(ot * ot, axis=0, keepdims=True)
        on = ot * lax.rsqrt(ms + EPS) * sub_ref[...] * (1.0 - LAMBDA_INIT)
        o_s[hs, :] = (on * sgt_ref[0, hs, :].astype(F32)).astype(BF16)
    out = lax.dot_general(o_s[...], wout_ref[...], (((0,), (0,)), ((), ())), preferred_element_type=F32)
    x2 = x1_ref[0] + mod_ref[pl.ds(b, 1), 2 * D:3 * D] * out
    y_ref[0] = _rms(x2, fng_ref[...])


def _attn_prompt(qt, kb, vtb, sgt, x1, mod, lamp, sub, wout, fng, *, tq):
    nb, t, _ = kb.shape
    tk = vtb.shape[-1]
    kern = functools.partial(_attn_prompt_kernel, tq=tq, tk=tk)
    row = pl.BlockSpec((1, tq, D), lambda b, i: (b, i, 0))
    col = pl.BlockSpec((1, D, tq), lambda b, i: (b, 0, i))
    return pl.pallas_call(
        kern,
        grid=(nb, t // tq),
        in_specs=[col,
                  pl.BlockSpec((1, t, D), lambda b, i: (b, 0, 0)),
                  pl.BlockSpec((1, t // tk, D, tk), lambda b, i: (b, 0, 0, 0)),
                  col, row, _const_spec(mod.shape), _const_spec(lamp.shape), _const_spec(sub.shape),
                  _const_spec(wout.shape), _const_spec(fng.shape)],
        out_specs=row,
        out_shape=jax.ShapeDtypeStruct((nb, t, D), F32),
        scratch_shapes=[pltpu.VMEM((D, tq), BF16),
                        pltpu.VMEM((N_HEADS, 2 * HEAD_DIM, 2 * tq), BF16),
                        pltpu.VMEM((N_HEADS, 1, 2 * tq), F32),
                        pltpu.VMEM((N_HEADS, 1, 2 * tq), F32),
                        pltpu.VMEM((N_HEADS, V_DIM, 2 * tq), F32),
                        pltpu.VMEM((N_HEADS, tk, 2 * tq), F32)],
        compiler_params=pltpu.CompilerParams(dimension_semantics=("arbitrary", "arbitrary"),
                                             vmem_limit_bytes=VMEM_LIMIT),
        name="attn_prompt",
    )(qt, kb, vtb, sgt, x1, mod, lamp, sub, wout, fng)


N_ACC = 8


def _attn_sample_kernel(pt_ref, qb_ref, q2_ref, kn_ref, vn_ref, lamp_ref, *refs, g):
    k_refs = refs[:g]
    v_refs = refs[g:2 * g]
    o_ref = refs[2 * g]
    m_s, l_s, acc_s = refs[2 * g + 1:]
    p = pl.program_id(1)

    @pl.when(p == 0)
    def _():
        for c in range(2):
            m_s[c] = jnp.sum(q2_ref[c] * kn_ref[c], axis=-1, keepdims=True)
            l_s[c] = jnp.ones((N_HEADS, 1), F32)
            acc_s[c] = vn_ref[...]

    probs = []
    corrs = []
    for c in range(2):
        qbc = qb_ref[c]
        s = jnp.concatenate([jnp.sum(k_refs[i][:, c] * qbc, axis=1) for i in range(g)], axis=1)
        m_old = m_s[c]
        m_new = jnp.maximum(m_old, jnp.max(s, axis=-1, keepdims=True))
        corr = jnp.exp(m_old - m_new)
        pr = jnp.exp(s - m_new)
        l_s[c] = l_s[c] * corr + jnp.sum(pr, axis=-1, keepdims=True)
        m_s[c] = m_new
        probs.append(pr)
        corrs.append(corr)
    p16 = jnp.concatenate(probs, axis=0).astype(BF16)
    accs = [[acc_s[c] * corrs[c]] + [jnp.zeros((N_HEADS, V_DIM), F32)] * (N_ACC - 1) for c in range(2)]
    for i in range(g):
        for pos in range(PAGE_SIZE):
            vrow = v_refs[i][pos]
            col = i * PAGE_SIZE + pos
            pb = jnp.broadcast_to(p16[:, col:col + 1], (2 * N_HEADS, V_DIM)).astype(F32)
            a = pos % N_ACC
            for c in range(2):
                accs[c][a] = accs[c][a] + pb[c * N_HEADS:(c + 1) * N_HEADS] * vrow
    for c in range(2):
        tot = accs[c][0]
        for a in range(1, N_ACC):
            tot = tot + accs[c][a]
        acc_s[c] = tot

    @pl.when(p == pl.num_programs(1) - 1)
    def _():
        lam = _diff_lambda(lamp_ref[...])
        o_ref[...] = acc_s[0] / l_s[0] - lam * (acc_s[1] / l_s[1])


def _attn_sample(page_table, qb, q2, kn, vn, lamp, ck5, cv, *, g):
    n, n_pages = page_table.shape
    kern = functools.partial(_attn_sample_kernel, g=g)

    def kmap(i):
        return lambda b, p, pt: (pt[b, p * g + i], 0, 0, 0, 0)

    def vmap_(i):
        return lambda b, p, pt: (pt[b, p * g + i], 0, 0, 0)

    k_specs = [pl.BlockSpec((None, N_HEADS, 2, HEAD_DIM, PAGE_SIZE), kmap(i)) for i in range(g)]
    v_specs = [pl.BlockSpec((None, PAGE_SIZE, N_HEADS, V_DIM), vmap_(i)) for i in range(g)]
    grid_spec = pltpu.PrefetchScalarGridSpec(
        num_scalar_prefetch=1,
        grid=(n, n_pages // g),
        in_specs=[pl.BlockSpec((None, 2, N_HEADS, HEAD_DIM, PAGE_SIZE), lambda b, p, pt: (b, 0, 0, 0, 0)),
                  pl.BlockSpec((None, 2, N_HEADS, HEAD_DIM), lambda b, p, pt: (b, 0, 0, 0)),
                  pl.BlockSpec((None, 2, N_HEADS, HEAD_DIM), lambda b, p, pt: (b, 0, 0, 0)),
                  pl.BlockSpec((None, N_HEADS, V_DIM), lambda b, p, pt: (b, 0, 0)),
                  pl.BlockSpec(lamp.shape, lambda b, p, pt: (0, 0))] + k_specs + v_specs,
        out_specs=pl.BlockSpec((None, N_HEADS, V_DIM), lambda b, p, pt: (b, 0, 0)),
        scratch_shapes=[pltpu.VMEM((2, N_HEADS, 1), F32), pltpu.VMEM((2, N_HEADS, 1), F32),
                        pltpu.VMEM((2, N_HEADS, V_DIM), F32)],
    )
    return pl.pallas_call(
        kern,
        grid_spec=grid_spec,
        out_shape=jax.ShapeDtypeStruct((n, N_HEADS, V_DIM), F32),
        compiler_params=pltpu.CompilerParams(dimension_semantics=("arbitrary", "arbitrary"),
                                             vmem_limit_bytes=VMEM_LIMIT),
        name="attn_sample",
    )(page_table, qb, q2, kn, vn, lamp, *([ck5] * g), *([cv] * g))


def _epi_sample_kernel(o_ref, sg_ref, x1_ref, mod_ref, sub_ref, wout_ref, fng_ref, y_ref):
    parts = []
    for h in range(N_HEADS):
        hs = slice(V_DIM * h, V_DIM * (h + 1))
        parts.append((_subln(o_ref[:, hs], sub_ref[...]) * sg_ref[:, hs]).astype(BF16))
    out = _dot(jnp.concatenate(parts, axis=1), wout_ref[...])
    x2 = x1_ref[...] + mod_ref[:, 2 * D:3 * D] * out
    y_ref[...] = _rms(x2, fng_ref[...])


def _epi_sample(o, sg, x1, mod, sub, wout, fng):
    return pl.pallas_call(
        _epi_sample_kernel,
        out_shape=jax.ShapeDtypeStruct(x1.shape, F32),
        name="epi_sample",
    )(o, sg, x1, mod, sub, wout, fng)


def _block_diag_slices(w):
    per = LRU_HEADS // N_SLICE
    w4 = w.reshape(N_SLICE, per, LRU_BLK, LRU_BLK)
    eye = jnp.eye(per, dtype=w.dtype)
    return jnp.einsum('skcd,kl->skcld', w4, eye).reshape(N_SLICE, SLICE_W, SLICE_W)


def _sliced(p):
    return p.reshape(p.shape[0], N_SLICE, SLICE_W).transpose(1, 0, 2)


def _l0_weights(w_in, conv_a_w, conv_a_b, wr, br, wi, bi, lam, conv_b_w, w_out):
    win = w_in.astype(BF16).reshape(D, 6, N_SLICE, SLICE_W).transpose(1, 2, 0, 3)
    wg = jnp.concatenate([_block_diag_slices(wr), _block_diag_slices(wi)], axis=-1).astype(BF16)
    wout = w_out.astype(BF16).reshape(2 * N_SLICE, SLICE_W, D)
    return (win, _sliced(conv_a_w), _sliced(conv_a_b[None]), wg, _sliced(br[None]), _sliced(bi[None]),
            _sliced(lam[None]), _sliced(conv_b_w), wout)


def _rope_lane_tables(pos, scale):
    half = ROT_DIM // 2
    inv = jnp.power(jnp.float32(ROPE_THETA), -jnp.arange(0, ROT_DIM, 2, dtype=F32) / ROT_DIM)
    ang = pos.astype(F32)[:, None] * inv[None, :]
    lane = jnp.arange(128)
    within = lane % HEAD_DIM
    cos = jnp.cos(ang)[:, lane % half]
    sin = jnp.sin(ang)[:, lane % half]
    c = jnp.where(within < ROT_DIM, cos, 1.0)
    sa = jnp.where((within >= half) & (within < ROT_DIM), sin, 0.0)
    sb = jnp.where(within < half, -sin, 0.0)
    return jnp.stack([c, sa, sb]) * scale


def kernel(x_prompt, x_sample, state_lru_h, state_conv_a, state_conv_b, cache_k, cache_v, page_table,
           c_prompt, c_sample, l0_norm_g, l0_ada_w, l0_ada_b, l0_w_in, l0_conv_a_w, l0_conv_a_b,
           l0_lru_wr, l0_lru_br, l0_lru_wi, l0_lru_bi, l0_lru_lam, l0_conv_b_w, l0_w_out,
           l1_norm_g, l1_ada_w, l1_ada_b, l1_w_in, l1_lam_q1, l1_lam_k1, l1_lam_q2, l1_lam_k2,
           l1_subln_g, l1_w_out, final_norm_g):
    nb, t, _ = x_prompt.shape
    ns = x_sample.shape[0]
    n_pages = page_table.shape[1]
    past_len = n_pages * PAGE_SIZE
    tq = 256

    c_all = jnp.concatenate([c_prompt, c_sample], axis=0)
    mod0 = _ada(c_all, l0_ada_w, l0_ada_b)
    mod1 = _ada(c_all, l1_ada_w, l1_ada_b)

    w0 = _l0_weights(l0_w_in, l0_conv_a_w, l0_conv_a_b, l0_lru_wr, l0_lru_br, l0_lru_wi, l0_lru_bi,
                     l0_lru_lam, l0_conv_b_w, l0_w_out)
    ng0 = l0_norm_g.reshape(1, D)
    x1p, hl_p, ca_p, cb_p = _l0_prompt(x_prompt, mod0[:nb], ng0, w0)
    x1s, hl_s, ca_s, cb_s = _l0_sample(x_sample.reshape(ns, D), mod0[nb:], ng0, state_lru_h,
                                       state_conv_a.transpose(1, 0, 2), state_conv_b.transpose(1, 0, 2), w0)
    lru_h_p = hl_p.transpose(1, 0, 2).reshape(nb, D)
    conv_a_p = ca_p.transpose(1, 2, 0, 3).reshape(nb, CONV_A - 1, D)
    conv_b_p = cb_p.transpose(1, 2, 0, 3).reshape(nb, CONV_B - 1, D)
    conv_a_s = ca_s.transpose(1, 0, 2)
    conv_b_s = cb_s.transpose(1, 0, 2)

    w1 = l1_w_in.astype(BF16)
    wq, wk, wv, wg = w1[:, 0:D], w1[:, D:2 * D], w1[:, 2 * D:3 * D], w1[:, 3 * D:4 * D]
    ng1 = l1_norm_g.reshape(1, D)
    pos_p = jnp.arange(t, dtype=jnp.int32)
    pos_s = jnp.full((1,), past_len, jnp.int32)
    half = ROT_DIM // 2
    inv = jnp.power(jnp.float32(ROPE_THETA), -jnp.arange(0, ROT_DIM, 2, dtype=F32) / ROT_DIM)
    ang_t = inv[:, None] * pos_p.astype(F32)[None, :]
    qt_p, kt_p, kb_p, v_p, vtb_p, sgt_p = _qkvg_prompt(
        x1p, mod1[:nb], ng1, wq.T, wk.T, wk, wv, wv.T, wg.T, _rope_lane_tables(pos_p, 1.0),
        jnp.cos(ang_t), jnp.sin(ang_t), tk=tq)
    q_s, k_s, v_s, sg_s = _qkvg_sample(x1s, mod1[nb:], ng1, wq, wk, wv, wg,
                                       _rope_lane_tables(pos_s, SCALE), _rope_lane_tables(pos_s, 1.0))

    lamp = jnp.stack([l1_lam_q1, l1_lam_k1, l1_lam_q2, l1_lam_k2])
    sub = l1_subln_g.reshape(1, V_DIM)
    wout1 = l1_w_out.astype(BF16)
    fng = final_norm_g.reshape(1, D)
    y_p = _attn_prompt(qt_p, kb_p, vtb_p, sgt_p, x1p, mod1[:nb], lamp, l1_subln_g.reshape(V_DIM, 1), wout1, fng,
                       tq=tq)

    q2 = q_s.reshape(ns, N_HEADS, 2, HEAD_DIM).transpose(0, 2, 1, 3)
    kn = k_s.reshape(ns, N_HEADS, 2, HEAD_DIM).transpose(0, 2, 1, 3)
    qb = jnp.broadcast_to(q2[..., None], (ns, 2, N_HEADS, HEAD_DIM, PAGE_SIZE))
    ck5 = cache_k.transpose(0, 2, 3, 1).reshape(cache_k.shape[0], N_HEADS, 2, HEAD_DIM, PAGE_SIZE)
    o_s = _attn_sample(page_table, qb, q2, kn, v_s.reshape(ns, N_HEADS, V_DIM), lamp, ck5, cache_v, g=8)
    y_s = _epi_sample(o_s.reshape(ns, D), sg_s, x1s, mod1[nb:], sub, wout1, fng)

    k_p = kt_p.transpose(0, 3, 1, 2)
    return (y_p, y_s.reshape(ns, 1, D), lru_h_p, hl_s, conv_a_p, conv_a_s, conv_b_p, conv_b_s,
            k_p, v_p.reshape(nb, t, N_HEADS, V_DIM), k_s.reshape(ns, 1, 2 * N_HEADS, HEAD_DIM),
            v_s.reshape(ns, 1, N_HEADS, V_DIM))
```

```python
import functools
import math

import jax
import jax.numpy as jnp
from jax import lax
from jax.experimental import pallas as pl
from jax.experimental.pallas import tpu as pltpu

F32 = jnp.float32
BF16 = jnp.bfloat16

D = 1024
LRU_HEADS = 16
LRU_BLK = D // LRU_HEADS
CONV_A = 4
CONV_B = 3
RG_C = 8.0
N_HEADS = 8
HEAD_DIM = 64
V_DIM = 128
ROT_DIM = 16
ROPE_THETA = 500000.0
LAMBDA_INIT = 0.8 - 0.6 * math.exp(-0.3 * 1)
SCALE = HEAD_DIM ** -0.5
LOG2E = math.log2(math.e)
PAGE_SIZE = 128
EPS = 1e-6
NEG = -1e30

N_SLICE = 4
SLICE_W = D // N_SLICE
CARRY_PAD = 8
VMEM_LIMIT = 56 * 1024 * 1024


def _silu(x):
    return x * jax.nn.sigmoid(x)


def _softplus(x):
    return jnp.maximum(x, 0.0) + jnp.log1p(jnp.exp(-jnp.abs(x)))


def _rms(x, g):
    return x * lax.rsqrt(jnp.mean(x * x, axis=-1, keepdims=True) + EPS) * g


def _dot(a, b):
    return jnp.dot(a, b, preferred_element_type=F32)


def _ada_kernel(c_ref, w_ref, b_ref, o_ref):
    s = _silu(c_ref[...]).astype(BF16)
    o_ref[...] = _dot(s, w_ref[...].astype(BF16)) + b_ref[...]


def _ada(c, w, b):
    n, tn = w.shape[1], 768
    return pl.pallas_call(
        _ada_kernel,
        grid=(n // tn,),
        in_specs=[pl.BlockSpec(c.shape, lambda j: (0, 0)),
                  pl.BlockSpec((D, tn), lambda j: (0, j)),
                  pl.BlockSpec((1, tn), lambda j: (0, j))],
        out_specs=pl.BlockSpec((c.shape[0], tn), lambda j: (0, j)),
        out_shape=jax.ShapeDtypeStruct((c.shape[0], n), F32),
        name="ada_mod",
    )(c, w, b.reshape(1, n))


def _lru_gates(xc, ri, br, bi, lam):
    w = xc.shape[1]
    r = jax.nn.sigmoid(ri[:, :w] + br)
    ig = jax.nn.sigmoid(ri[:, w:] + bi)
    log_a = (-RG_C * _softplus(-lam)) * r
    a = jnp.exp(log_a)
    u = jnp.sqrt(1.0 - jnp.exp(2.0 * log_a)) * (ig * xc)
    return a, u


def _l0_prompt_kernel(x_ref, mod_ref, ng_ref, win_ref, caw_ref, cab_ref, wg_ref, br_ref, bi_ref,
                      lam_ref, cbw_ref, wout_ref,
                      x1_ref, hl_ref, ca_ref, cb_ref,
                      xk_s, hbc_s, hbn_s, za_s, zb_s, oacc_s, exta_s, extb_s, xc_s, y_s, h_s, *, nb, tb):
    k = pl.program_id(0)
    pa = CARRY_PAD - (CONV_A - 1)
    pb = CARRY_PAD - (CONV_B - 1)
    w = SLICE_W

    def norm_incoming(dst):
        for b in range(nb):
            hn = _rms(x_ref[b], ng_ref[...])
            hn = hn * (1.0 + mod_ref[b:b + 1, D:2 * D]) + mod_ref[b:b + 1, 0:D]
            dst[b * tb:(b + 1) * tb, :] = hn.astype(BF16)

    @pl.when(k == 0)
    def _():
        exta_s[:, :, 0:CARRY_PAD, :] = jnp.zeros((N_SLICE, nb, CARRY_PAD, w), F32)
        extb_s[:, :, 0:CARRY_PAD, :] = jnp.zeros((N_SLICE, nb, CARRY_PAD, w), F32)
        h_s[...] = jnp.zeros(h_s.shape, F32)
        norm_incoming(hbn_s)
        za_s[...] = _dot(hbn_s[...], win_ref[0])
        xk_s[...] = x_ref[...]

    row8 = lax.broadcasted_iota(jnp.int32, (tb, w), 0) & 7

    def out_chunk(s):
        return _dot(y_s[s], wout_ref[s])

    def slice_stage(s, zc, zn):
        if s == 1:
            oacc_s[...] = out_chunk(0)
        elif s > 1:
            oacc_s[...] = oacc_s[...] + out_chunk(s - 1)
        caw = caw_ref[s]
        for b in range(nb):
            exta_s[s, b, CARRY_PAD:CARRY_PAD + tb, :] = zc[b * tb:(b + 1) * tb, 0:w]
            acc = caw[0:1] * exta_s[s, b, pl.ds(pa, tb), :]
            for j in range(1, CONV_A):
                acc = acc + caw[j:j + 1] * exta_s[s, b, pl.ds(pa + j, tb), :]
            xc_s[b * tb:(b + 1) * tb, :] = acc + cab_ref[s]
            tail = exta_s[s, b, pl.ds(pa + tb, CONV_A - 1), :]
            exta_s[s, b, pa:CARRY_PAD, :] = tail
            ca_ref[s, b] = tail
        xc = xc_s[...]
        ri = _dot(xc.astype(BF16), wg_ref[s])
        if s + 1 < N_SLICE:
            zn[...] = _dot(hbc_s[...], win_ref[s + 1])
        else:
            zn[...] = _dot(hbn_s[...], win_ref[0])
        a, u = _lru_gates(xc, ri, br_ref[s], bi_ref[s], lam_ref[s])
        for b in range(nb):
            rows = slice(b * tb, (b + 1) * tb)
            ab = a[rows]
            ub = u[rows]
            for d in (1, 2, 4):
                keep = row8 >= d
                a_sh = jnp.where(keep, pltpu.roll(ab, d, axis=0), 1.0)
                u_sh = jnp.where(keep, pltpu.roll(ub, d, axis=0), 0.0)
                ub = ab * u_sh + ub
                ab = ab * a_sh
            hprev = h_s[s, b:b + 1, :]
            hs = []
            for g in range(tb // 8):
                hg = ub[8 * g:8 * g + 8] + ab[8 * g:8 * g + 8] * hprev
                hprev = hg[7:8]
                hs.append(hg)
            h_s[s, b:b + 1, :] = hprev
            hl_ref[s, b:b + 1, :] = hprev
            ya = jnp.concatenate(hs, axis=0) * _silu(zc[rows, w:2 * w])
            y_s[s, rows, 0:w] = ya.astype(BF16)
        cbw = cbw_ref[s]
        for b in range(nb):
            rows = slice(b * tb, (b + 1) * tb)
            extb_s[s, b, CARRY_PAD:CARRY_PAD + tb, :] = zc[rows, 3 * w:4 * w] * zc[rows, 4 * w:5 * w]
            uc = cbw[0:1] * extb_s[s, b, pl.ds(pb, tb), :]
            for j in range(1, CONV_B):
                uc = uc + cbw[j:j + 1] * extb_s[s, b, pl.ds(pb + j, tb), :]
            tail = extb_s[s, b, pl.ds(pb + tb, CONV_B - 1), :]
            extb_s[s, b, pb:CARRY_PAD, :] = tail
            cb_ref[s, b] = tail
            yb = zc[rows, 2 * w:3 * w] * uc * _silu(zc[rows, 5 * w:6 * w])
            y_s[s, rows, w:2 * w] = yb.astype(BF16)

    @pl.when(k > 0)
    def _():
        hbc_s[...] = hbn_s[...]
        norm_incoming(hbn_s)
        for s in range(N_SLICE):
            zc, zn = (za_s, zb_s) if s % 2 == 0 else (zb_s, za_s)
            slice_stage(s, zc, zn)
        out = oacc_s[...] + out_chunk(N_SLICE - 1)
        for b in range(nb):
            x1_ref[b] = xk_s[b] + mod_ref[b:b + 1, 2 * D:3 * D] * out[b * tb:(b + 1) * tb]
        xk_s[...] = x_ref[...]


def _const_spec(shape):
    nd = len(shape)
    return pl.BlockSpec(shape, lambda *_: (0,) * nd, pipeline_mode=pl.Buffered(1))


def _l0_prompt(x, mod, ng, w):
    nb, t, _ = x.shape
    tb = 64
    m = nb * tb
    nt = t // tb
    kern = functools.partial(_l0_prompt_kernel, nb=nb, tb=tb)
    wshapes = [a.shape for a in w]
    return pl.pallas_call(
        kern,
        grid=(nt + 1,),
        in_specs=[pl.BlockSpec((nb, tb, D), lambda k: (0, jnp.minimum(k, nt - 1), 0)),
                  _const_spec(mod.shape), _const_spec(ng.shape)] + [_const_spec(s) for s in wshapes],
        out_specs=[pl.BlockSpec((nb, tb, D), lambda k: (0, jnp.maximum(k - 1, 0), 0)),
                   pl.BlockSpec((N_SLICE, nb, SLICE_W), lambda k: (0, 0, 0)),
                   pl.BlockSpec((N_SLICE, nb, CONV_A - 1, SLICE_W), lambda k: (0, 0, 0, 0)),
                   pl.BlockSpec((N_SLICE, nb, CONV_B - 1, SLICE_W), lambda k: (0, 0, 0, 0))],
        out_shape=[jax.ShapeDtypeStruct((nb, t, D), F32),
                   jax.ShapeDtypeStruct((N_SLICE, nb, SLICE_W), F32),
                   jax.ShapeDtypeStruct((N_SLICE, nb, CONV_A - 1, SLICE_W), F32),
                   jax.ShapeDtypeStruct((N_SLICE, nb, CONV_B - 1, SLICE_W), F32)],
        scratch_shapes=[pltpu.VMEM((nb, tb, D), F32),
                        pltpu.VMEM((m, D), BF16),
                        pltpu.VMEM((m, D), BF16),
                        pltpu.VMEM((m, 6 * SLICE_W), F32),
                        pltpu.VMEM((m, 6 * SLICE_W), F32),
                        pltpu.VMEM((m, D), F32),
                        pltpu.VMEM((N_SLICE, nb, CARRY_PAD + tb, SLICE_W), F32),
                        pltpu.VMEM((N_SLICE, nb, CARRY_PAD + tb, SLICE_W), F32),
                        pltpu.VMEM((m, SLICE_W), F32),
                        pltpu.VMEM((N_SLICE, m, 2 * SLICE_W), BF16),
                        pltpu.VMEM((N_SLICE, nb, SLICE_W), F32)],
        compiler_params=pltpu.CompilerParams(dimension_semantics=("arbitrary",),
                                             vmem_limit_bytes=VMEM_LIMIT),
        name="l0_prompt",
    )(x, mod, ng, *w)


def _l0_sample_kernel(x_ref, mod_ref, ng_ref, h0_ref, ca0_ref, cb0_ref, win_ref, caw_ref, cab_ref,
                      wg_ref, br_ref, bi_ref, lam_ref, cbw_ref, wout_ref,
                      x1_ref, hl_ref, ca_ref, cb_ref):
    x = x_ref[...]
    hn = _rms(x, ng_ref[...])
    hb = (hn * (1.0 + mod_ref[:, D:2 * D]) + mod_ref[:, 0:D]).astype(BF16)
    out = jnp.zeros(x.shape, F32)
    for s in range(N_SLICE):
        sl = slice(s * SLICE_W, (s + 1) * SLICE_W)
        z = _dot(hb, win_ref[s])
        xa = z[:, 0:SLICE_W]
        caw = caw_ref[s]
        xc = caw[CONV_A - 1:CONV_A] * xa + cab_ref[s]
        for j in range(CONV_A - 1):
            xc = xc + caw[j:j + 1] * ca0_ref[j, :, sl]
        for j in range(CONV_A - 2):
            ca_ref[j, :, sl] = ca0_ref[j + 1, :, sl]
        ca_ref[CONV_A - 2, :, sl] = xa
        ri = _dot(xc.astype(BF16), wg_ref[s])
        a, u = _lru_gates(xc, ri, br_ref[s], bi_ref[s], lam_ref[s])
        h = a * h0_ref[:, sl] + u
        hl_ref[:, sl] = h
        ya = h * _silu(z[:, SLICE_W:2 * SLICE_W])
        gate_b = z[:, 2 * SLICE_W:3 * SLICE_W]
        u2 = z[:, 3 * SLICE_W:4 * SLICE_W] * z[:, 4 * SLICE_W:5 * SLICE_W]
        gsil = _silu(z[:, 5 * SLICE_W:6 * SLICE_W])
        cbw = cbw_ref[s]
        uc = cbw[CONV_B - 1:CONV_B] * u2
        for j in range(CONV_B - 1):
            uc = uc + cbw[j:j + 1] * cb0_ref[j, :, sl]
        for j in range(CONV_B - 2):
            cb_ref[j, :, sl] = cb0_ref[j + 1, :, sl]
        cb_ref[CONV_B - 2, :, sl] = u2
        yb = gate_b * uc * gsil
        out = out + _dot(jnp.concatenate([ya, yb], axis=1).astype(BF16), wout_ref[s])
    x1_ref[...] = x + mod_ref[:, 2 * D:3 * D] * out


def _l0_sample(x, mod, ng, h0, ca0, cb0, w):
    n = x.shape[0]
    return pl.pallas_call(
        _l0_sample_kernel,
        out_shape=[jax.ShapeDtypeStruct((n, D), F32), jax.ShapeDtypeStruct((n, D), F32),
                   jax.ShapeDtypeStruct((CONV_A - 1, n, D), F32),
                   jax.ShapeDtypeStruct((CONV_B - 1, n, D), F32)],
        compiler_params=pltpu.CompilerParams(vmem_limit_bytes=VMEM_LIMIT),
        name="l0_sample",
    )(x, mod, ng, h0, ca0, cb0, *w)


def _rope_lanes(x, tab_ref):
    outs = []
    for j in range(D // 128):
        xj = x[:, 128 * j:128 * (j + 1)]
        outs.append(xj * tab_ref[0] + pltpu.roll(xj, 8, axis=1) * tab_ref[1]
                    + pltpu.roll(xj, 120, axis=1) * tab_ref[2])
    return outs


def _rope_rows(xt, c, s, scale):
    x3 = xt.reshape(2 * N_HEADS, HEAD_DIM, xt.shape[1])
    half = ROT_DIM // 2
    x1 = x3[:, 0:half, :]
    x2 = x3[:, half:ROT_DIM, :]
    rest = x3[:, ROT_DIM:, :]
    if scale != 1.0:
        rest = rest * scale
    return jnp.concatenate([x1 * c - x2 * s, x2 * c + x1 * s, rest], axis=1)


_NT = (((1,), (1,)), ((), ()))


def _qkvg_prompt_kernel(x_ref, mod_ref, ng_ref, wqt_ref, wkt_ref, wk_ref, wv_ref, wvt_ref, wgt_ref,
                        tk_ref, ck_ref, sk_ref,
                        qt_ref, kt_ref, kb_ref, v_ref, vtb_ref, sgt_ref, *, tb, tk):
    b = pl.program_id(0)
    hn = _rms(x_ref[0], ng_ref[...])
    hb = (hn * (1.0 + mod_ref[pl.ds(b, 1), D:2 * D]) + mod_ref[pl.ds(b, 1), 0:D]).astype(BF16)
    ck = ck_ref[...]
    sk = sk_ref[...]
    qs = SCALE * LOG2E
    qt = lax.dot_general(wqt_ref[...], hb, _NT, preferred_element_type=F32)
    qt_ref[0] = _rope_rows(qt, ck * qs, sk * qs, qs).reshape(D, tb).astype(BF16)
    kt = lax.dot_general(wkt_ref[...], hb, _NT, preferred_element_type=F32)
    kt_ref[0] = _rope_rows(kt, ck, sk, 1.0)
    for j, kj in enumerate(_rope_lanes(_dot(hb, wk_ref[...]), tk_ref)):
        kb_ref[0, :, 128 * j:128 * (j + 1)] = kj.astype(BF16)
    v_ref[0] = _dot(hb, wv_ref[...])
    vt = lax.dot_general(wvt_ref[...], hb, _NT, preferred_element_type=F32).astype(BF16)
    for j in range(tb // tk):
        vtb_ref[0, j] = vt[:, j * tk:(j + 1) * tk]
    sgt_ref[0] = _silu(lax.dot_general(wgt_ref[...], hb, _NT, preferred_element_type=F32)).astype(BF16)


def _qkvg_prompt(x1, mod, ng, wqt, wkt, wk, wv, wvt, wgt, tabk, ck, sk, *, tk):
    nb, t, _ = x1.shape
    tb = 512
    kern = functools.partial(_qkvg_prompt_kernel, tb=tb, tk=tk)
    row = pl.BlockSpec((1, tb, D), lambda b, i: (b, i, 0))
    col = pl.BlockSpec((1, D, tb), lambda b, i: (b, 0, i))
    wspec = _const_spec(wqt.shape)
    return pl.pallas_call(
        kern,
        grid=(nb, t // tb),
        in_specs=[row, _const_spec(mod.shape), _const_spec(ng.shape)] + [wspec] * 6 + [
            pl.BlockSpec((3, tb, 128), lambda b, i: (0, i, 0)),
            pl.BlockSpec((ROT_DIM // 2, tb), lambda b, i: (0, i)),
            pl.BlockSpec((ROT_DIM // 2, tb), lambda b, i: (0, i))],
        out_specs=[col,
                   pl.BlockSpec((1, 2 * N_HEADS, HEAD_DIM, tb), lambda b, i: (b, 0, 0, i)),
                   row, row,
                   pl.BlockSpec((1, tb // tk, D, tk), lambda b, i: (b, i, 0, 0)),
                   col],
        out_shape=[jax.ShapeDtypeStruct((nb, D, t), BF16),
                   jax.ShapeDtypeStruct((nb, 2 * N_HEADS, HEAD_DIM, t), F32),
                   jax.ShapeDtypeStruct((nb, t, D), BF16),
                   jax.ShapeDtypeStruct((nb, t, D), F32),
                   jax.ShapeDtypeStruct((nb, t // tk, D, tk), BF16),
                   jax.ShapeDtypeStruct((nb, D, t), BF16)],
        compiler_params=pltpu.CompilerParams(dimension_semantics=("arbitrary", "arbitrary"),
                                             vmem_limit_bytes=VMEM_LIMIT),
        name="qkvg_prompt",
    )(x1, mod, ng, wqt, wkt, wk, wv, wvt, wgt, tabk, ck, sk)


def _qkvg_sample_kernel(x_ref, mod_ref, ng_ref, wq_ref, wk_ref, wv_ref, wg_ref, tq_ref, tk_ref,
                        q_ref, k_ref, v_ref, sg_ref):
    hn = _rms(x_ref[...], ng_ref[...])
    hb = (hn * (1.0 + mod_ref[:, D:2 * D]) + mod_ref[:, 0:D]).astype(BF16)
    for j, qj in enumerate(_rope_lanes(_dot(hb, wq_ref[...]), tq_ref)):
        q_ref[:, 128 * j:128 * (j + 1)] = qj
    for j, kj in enumerate(_rope_lanes(_dot(hb, wk_ref[...]), tk_ref)):
        k_ref[:, 128 * j:128 * (j + 1)] = kj
    v_ref[...] = _dot(hb, wv_ref[...])
    sg_ref[...] = _silu(_dot(hb, wg_ref[...]))


def _qkvg_sample(x1, mod, ng, wq, wk, wv, wg, tabq, tabk):
    n = x1.shape[0]
    return pl.pallas_call(
        _qkvg_sample_kernel,
        out_shape=[jax.ShapeDtypeStruct((n, D), F32)] * 4,
        compiler_params=pltpu.CompilerParams(vmem_limit_bytes=VMEM_LIMIT),
        name="qkvg_sample",
    )(x1, mod, ng, wq, wk, wv, wg, tabq, tabk)


def _diff_lambda(lp):
    e1 = jnp.exp(jnp.sum(lp[0:1] * lp[1:2], axis=-1, keepdims=True))
    e2 = jnp.exp(jnp.sum(lp[2:3] * lp[3:4], axis=-1, keepdims=True))
    return e1 - e2 + LAMBDA_INIT


def _subln(o, sub):
    return _rms(o, sub) * (1.0 - LAMBDA_INIT)


def _attn_prompt_kernel(qt_ref, kb_ref, vt_ref, sgt_ref, x1_ref, mod_ref, lamp_ref, sub_ref, wout_ref, fng_ref,
                        y_ref, o_s, qq_s, m_s, l_s, acc_s, st_s, *, tq, tk):
    b = pl.program_id(0)
    qi = pl.program_id(1)
    row = lax.broadcasted_iota(jnp.int32, (2 * HEAD_DIM, tq), 0)
    for h in range(N_HEADS):
        qh = qt_ref[0, V_DIM * h:V_DIM * (h + 1), :]
        zero = jnp.zeros_like(qh)
        qq_s[h, :, 0:tq] = jnp.where(row < HEAD_DIM, qh, zero)
        qq_s[h, :, tq:2 * tq] = jnp.where(row >= HEAD_DIM, qh, zero)
    m_s[...] = jnp.full(m_s.shape, NEG, F32)
    l_s[...] = jnp.zeros(l_s.shape, F32)
    acc_s[...] = jnp.zeros(acc_s.shape, F32)

    def scores(j, h):
        hs = slice(V_DIM * h, V_DIM * (h + 1))
        return _dot(kb_ref[0, pl.ds(pl.multiple_of(j * tk, tk), tk), hs], qq_s[h])

    def update(j, h, st, masked):
        if masked:
            keyi = lax.broadcasted_iota(jnp.int32, (tk, 2 * tq), 0)
            qryi = lax.broadcasted_iota(jnp.int32, (tk, 2 * tq), 1) & (tq - 1)
            st = jnp.where(keyi <= qryi, st, NEG)
        m_old = m_s[h]
        m_new = jnp.maximum(m_old, jnp.max(st, axis=0, keepdims=True))
        corr = jnp.exp2(m_old - m_new)
        p = jnp.exp2(st - m_new)
        m_s[h] = m_new
        l_s[h] = l_s[h] * corr + jnp.sum(p, axis=0, keepdims=True)
        acc_s[h] = acc_s[h] * corr + _dot(vt_ref[0, j, V_DIM * h:V_DIM * (h + 1), :], p.astype(BF16))

    for h in range(N_HEADS):
        st_s[h] = scores(0, h)

    def body(j, carry):
        for h in range(N_HEADS):
            cur = st_s[h]
            nxt = scores(j + 1, h)
            update(j, h, cur, False)
            st_s[h] = nxt
        return carry

    lax.fori_loop(0, qi, body, 0)
    for h in range(N_HEADS):
        update(qi, h, st_s[h], True)

    lam = _diff_lambda(lamp_ref[...])
    for h in range(N_HEADS):
        hs = slice(V_DIM * h, V_DIM * (h + 1))
        o12 = acc_s[h] / l_s[h]
        ot = o12[:, :tq] - lam * o12[:, tq:]
        ms = jnp.mean(ot * ot, axis=0, keepdims=True)
        on = ot * lax.rsqrt(ms + EPS) * sub_ref[...] * (1.0 - LAMBDA_INIT)
        o_s[hs, :] = (on * sgt_ref[0, hs, :].astype(F32)).astype(BF16)
    out = lax.dot_general(o_s[...], wout_ref[...], (((0,), (0,)), ((), ())), preferred_element_type=F32)
    x2 = x1_ref[0] + mod_ref[pl.ds(b, 1), 2 * D:3 * D] * out
    y_ref[0] = _rms(x2, fng_ref[...])


def _attn_prompt(qt, kb, vtb, sgt, x1, mod, lamp, sub, wout, fng, *, tq):
    nb, t, _ = kb.shape
    tk = vtb.shape[-1]
    kern = functools.partial(_attn_prompt_kernel, tq=tq, tk=tk)
    row = pl.BlockSpec((1, tq, D), lambda b, i: (b, i, 0))
    col = pl.BlockSpec((1, D, tq), lambda b, i: (b, 0, i))
    return pl.pallas_call(
        kern,
        grid=(nb, t // tq),
        in_specs=[col,
                  pl.BlockSpec((1, t, D), lambda b, i: (b, 0, 0)),
                  pl.BlockSpec((1, t // tk, D, tk), lambda b, i: (b, 0, 0, 0)),
                  col, row, _const_spec(mod.shape), _const_spec(lamp.shape), _const_spec(sub.shape),
                  _const_spec(wout.shape), _const_spec(fng.shape)],
        out_specs=row,
        out_shape=jax.ShapeDtypeStruct((nb, t, D), F32),
        scratch_shapes=[pltpu.VMEM((D, tq), BF16),
                        pltpu.VMEM((N_HEADS, 2 * HEAD_DIM, 2 * tq), BF16),
                        pltpu.VMEM((N_HEADS, 1, 2 * tq), F32),
                        pltpu.VMEM((N_HEADS, 1, 2 * tq), F32),
                        pltpu.VMEM((N_HEADS, V_DIM, 2 * tq), F32),
                        pltpu.VMEM((N_HEADS, tk, 2 * tq), F32)],
        compiler_params=pltpu.CompilerParams(dimension_semantics=("arbitrary", "arbitrary"),
                                             vmem_limit_bytes=VMEM_LIMIT),
        name="attn_prompt",
    )(qt, kb, vtb, sgt, x1, mod, lamp, sub, wout, fng)


def _attn_sample_kernel(pt_ref, qb_ref, q2_ref, kn_ref, vn_ref, lamp_ref, rep_ref, *refs, g):
    k_refs = refs[:g]
    v_refs = refs[g:2 * g]
    o_ref = refs[2 * g]
    m_s, l_s, acc_s = refs[2 * g + 1:]
    p = pl.program_id(1)
    nr = 2 * N_HEADS

    @pl.when(p == 0)
    def _():
        m_s[...] = jnp.sum(q2_ref[...] * kn_ref[...], axis=-1, keepdims=True)
        l_s[...] = jnp.ones((nr, 1), F32)
        acc_s[...] = jnp.concatenate([vn_ref[...], vn_ref[...]], axis=0)

    rows = []
    for i in range(g):
        for c in range(2):
            rows.append(jnp.sum(k_refs[i][:, c] * qb_ref[c], axis=1))
    s = jnp.concatenate(rows, axis=0)
    m_blk = jnp.max(jnp.max(s, axis=-1, keepdims=True).reshape(g, nr, 1), axis=0)
    m_old = m_s[...]
    m_new = jnp.maximum(m_old, m_blk)
    corr = jnp.exp(m_old - m_new)
    pr = jnp.exp(s - jnp.tile(m_new, (g, 1)))
    l_s[...] = l_s[...] * corr + jnp.sum(jnp.sum(pr, axis=-1, keepdims=True).reshape(g, nr, 1), axis=0)
    m_s[...] = m_new
    pexp = _dot(pr.astype(BF16), rep_ref[...])
    lane = lax.broadcasted_iota(jnp.int32, pexp.shape, 1)
    row = lax.broadcasted_iota(jnp.int32, pexp.shape, 0)
    pexp = jnp.where((lane & (N_HEADS - 1)) == (row & (N_HEADS - 1)), pexp, 0.0).astype(BF16)
    acc = acc_s[...] * corr
    for i in range(g):
        v2 = v_refs[i][...].reshape(PAGE_SIZE * N_HEADS, V_DIM).astype(BF16)
        acc = acc + _dot(pexp[nr * i:nr * (i + 1)], v2)
    acc_s[...] = acc

    @pl.when(p == pl.num_programs(1) - 1)
    def _():
        lam = _diff_lambda(lamp_ref[...])
        o12 = acc_s[...] / l_s[...]
        o_ref[...] = o12[:N_HEADS] - lam * o12[N_HEADS:]


def _attn_sample(page_table, qb, q2, kn, vn, lamp, ck5, cv, *, g):
    rep = (jnp.arange(PAGE_SIZE * N_HEADS)[None, :] // N_HEADS == jnp.arange(PAGE_SIZE)[:, None]).astype(BF16)
    n, n_pages = page_table.shape
    kern = functools.partial(_attn_sample_kernel, g=g)

    def kmap(i):
        return lambda b, p, pt: (pt[b, p * g + i], 0, 0, 0, 0)

    def vmap_(i):
        return lambda b, p, pt: (pt[b, p * g + i], 0, 0, 0)

    k_specs = [pl.BlockSpec((None, N_HEADS, 2, HEAD_DIM, PAGE_SIZE), kmap(i)) for i in range(g)]
    v_specs = [pl.BlockSpec((None, PAGE_SIZE, N_HEADS, V_DIM), vmap_(i)) for i in range(g)]
    grid_spec = pltpu.PrefetchScalarGridSpec(
        num_scalar_prefetch=1,
        grid=(n, n_pages // g),
        in_specs=[pl.BlockSpec((None, 2, N_HEADS, HEAD_DIM, PAGE_SIZE), lambda b, p, pt: (b, 0, 0, 0, 0)),
                  pl.BlockSpec((None, 2 * N_HEADS, HEAD_DIM), lambda b, p, pt: (b, 0, 0)),
                  pl.BlockSpec((None, 2 * N_HEADS, HEAD_DIM), lambda b, p, pt: (b, 0, 0)),
                  pl.BlockSpec((None, N_HEADS, V_DIM), lambda b, p, pt: (b, 0, 0)),
                  pl.BlockSpec(lamp.shape, lambda b, p, pt: (0, 0)),
                  pl.BlockSpec(rep.shape, lambda b, p, pt: (0, 0))] + k_specs + v_specs,
        out_specs=pl.BlockSpec((None, N_HEADS, V_DIM), lambda b, p, pt: (b, 0, 0)),
        scratch_shapes=[pltpu.VMEM((2 * N_HEADS, 1), F32), pltpu.VMEM((2 * N_HEADS, 1), F32),
                        pltpu.VMEM((2 * N_HEADS, V_DIM), F32)],
    )
    return pl.pallas_call(
        kern,
        grid_spec=grid_spec,
        out_shape=jax.ShapeDtypeStruct((n, N_HEADS, V_DIM), F32),
        compiler_params=pltpu.CompilerParams(dimension_semantics=("arbitrary", "arbitrary"),
                                             vmem_limit_bytes=VMEM_LIMIT),
        name="attn_sample",
    )(page_table, qb, q2.reshape(n, 2 * N_HEADS, HEAD_DIM), kn.reshape(n, 2 * N_HEADS, HEAD_DIM), vn, lamp, rep,
      *([ck5] * g), *([cv] * g))


def _epi_sample_kernel(o_ref, sg_ref, x1_ref, mod_ref, sub_ref, wout_ref, fng_ref, y_ref):
    parts = []
    for h in range(N_HEADS):
        hs = slice(V_DIM * h, V_DIM * (h + 1))
        parts.append((_subln(o_ref[:, hs], sub_ref[...]) * sg_ref[:, hs]).astype(BF16))
    out = _dot(jnp.concatenate(parts, axis=1), wout_ref[...])
    x2 = x1_ref[...] + mod_ref[:, 2 * D:3 * D] * out
    y_ref[...] = _rms(x2, fng_ref[...])


def _epi_sample(o, sg, x1, mod, sub, wout, fng):
    return pl.pallas_call(
        _epi_sample_kernel,
        out_shape=jax.ShapeDtypeStruct(x1.shape, F32),
        name="epi_sample",
    )(o, sg, x1, mod, sub, wout, fng)


def _block_diag_slices(w):
    per = LRU_HEADS // N_SLICE
    w4 = w.reshape(N_SLICE, per, LRU_BLK, LRU_BLK)
    eye = jnp.eye(per, dtype=w.dtype)
    return jnp.einsum('skcd,kl->skcld', w4, eye).reshape(N_SLICE, SLICE_W, SLICE_W)


def _sliced(p):
    return p.reshape(p.shape[0], N_SLICE, SLICE_W).transpose(1, 0, 2)


def _l0_weights(w_in, conv_a_w, conv_a_b, wr, br, wi, bi, lam, conv_b_w, w_out):
    win = w_in.astype(BF16).reshape(D, 6, N_SLICE, SLICE_W).transpose(2, 0, 1, 3).reshape(N_SLICE, D, 6 * SLICE_W)
    wg = jnp.concatenate([_block_diag_slices(wr), _block_diag_slices(wi)], axis=-1).astype(BF16)
    wout = w_out.astype(BF16).reshape(2, N_SLICE, SLICE_W, D).transpose(1, 0, 2, 3).reshape(N_SLICE, 2 * SLICE_W, D)
    return (win, _sliced(conv_a_w), _sliced(conv_a_b[None]), wg, _sliced(br[None]), _sliced(bi[None]),
            _sliced(lam[None]), _sliced(conv_b_w), wout)


def _rope_lane_tables(pos, scale):
    half = ROT_DIM // 2
    inv = jnp.power(jnp.float32(ROPE_THETA), -jnp.arange(0, ROT_DIM, 2, dtype=F32) / ROT_DIM)
    ang = pos.astype(F32)[:, None] * inv[None, :]
    lane = jnp.arange(128)
    within = lane % HEAD_DIM
    cos = jnp.cos(ang)[:, lane % half]
    sin = jnp.sin(ang)[:, lane % half]
    c = jnp.where(within < ROT_DIM, cos, 1.0)
    sa = jnp.where((within >= half) & (within < ROT_DIM), sin, 0.0)
    sb = jnp.where(within < half, -sin, 0.0)
    return jnp.stack([c, sa, sb]) * scale


def kernel(x_prompt, x_sample, state_lru_h, state_conv_a, state_conv_b, cache_k, cache_v, page_table,
           c_prompt, c_sample, l0_norm_g, l0_ada_w, l0_ada_b, l0_w_in, l0_conv_a_w, l0_conv_a_b,
           l0_lru_wr, l0_lru_br, l0_lru_wi, l0_lru_bi, l0_lru_lam, l0_conv_b_w, l0_w_out,
           l1_norm_g, l1_ada_w, l1_ada_b, l1_w_in, l1_lam_q1, l1_lam_k1, l1_lam_q2, l1_lam_k2,
           l1_subln_g, l1_w_out, final_norm_g):
    nb, t, _ = x_prompt.shape
    ns = x_sample.shape[0]
    n_pages = page_table.shape[1]
    past_len = n_pages * PAGE_SIZE
    tq = 256

    c_all = jnp.concatenate([c_prompt, c_sample], axis=0)
    mod0 = _ada(c_all, l0_ada_w, l0_ada_b)
    mod1 = _ada(c_all, l1_ada_w, l1_ada_b)

    w0 = _l0_weights(l0_w_in, l0_conv_a_w, l0_conv_a_b, l0_lru_wr, l0_lru_br, l0_lru_wi, l0_lru_bi,
                     l0_lru_lam, l0_conv_b_w, l0_w_out)
    ng0 = l0_norm_g.reshape(1, D)
    x1p, hl_p, ca_p, cb_p = _l0_prompt(x_prompt, mod0[:nb], ng0, w0)
    x1s, hl_s, ca_s, cb_s = _l0_sample(x_sample.reshape(ns, D), mod0[nb:], ng0, state_lru_h,
                                       state_conv_a.transpose(1, 0, 2), state_conv_b.transpose(1, 0, 2), w0)
    lru_h_p = hl_p.transpose(1, 0, 2).reshape(nb, D)
    conv_a_p = ca_p.transpose(1, 2, 0, 3).reshape(nb, CONV_A - 1, D)
    conv_b_p = cb_p.transpose(1, 2, 0, 3).reshape(nb, CONV_B - 1, D)
    conv_a_s = ca_s.transpose(1, 0, 2)
    conv_b_s = cb_s.transpose(1, 0, 2)

    w1 = l1_w_in.astype(BF16)
    wq, wk, wv, wg = w1[:, 0:D], w1[:, D:2 * D], w1[:, 2 * D:3 * D], w1[:, 3 * D:4 * D]
    ng1 = l1_norm_g.reshape(1, D)
    pos_p = jnp.arange(t, dtype=jnp.int32)
    pos_s = jnp.full((1,), past_len, jnp.int32)
    half = ROT_DIM // 2
    inv = jnp.power(jnp.float32(ROPE_THETA), -jnp.arange(0, ROT_DIM, 2, dtype=F32) / ROT_DIM)
    ang_t = inv[:, None] * pos_p.astype(F32)[None, :]
    qt_p, kt_p, kb_p, v_p, vtb_p, sgt_p = _qkvg_prompt(
        x1p, mod1[:nb], ng1, wq.T, wk.T, wk, wv, wv.T, wg.T, _rope_lane_tables(pos_p, 1.0),
        jnp.cos(ang_t), jnp.sin(ang_t), tk=tq)
    q_s, k_s, v_s, sg_s = _qkvg_sample(x1s, mod1[nb:], ng1, wq, wk, wv, wg,
                                       _rope_lane_tables(pos_s, SCALE), _rope_lane_tables(pos_s, 1.0))

    lamp = jnp.stack([l1_lam_q1, l1_lam_k1, l1_lam_q2, l1_lam_k2])
    sub = l1_subln_g.reshape(1, V_DIM)
    wout1 = l1_w_out.astype(BF16)
    fng = final_norm_g.reshape(1, D)
    y_p = _attn_prompt(qt_p, kb_p, vtb_p, sgt_p, x1p, mod1[:nb], lamp, l1_subln_g.reshape(V_DIM, 1), wout1, fng,
                       tq=tq)

    q2 = q_s.reshape(ns, N_HEADS, 2, HEAD_DIM).transpose(0, 2, 1, 3)
    kn = k_s.reshape(ns, N_HEADS, 2, HEAD_DIM).transpose(0, 2, 1, 3)
    qb = jnp.broadcast_to(q2[..., None], (ns, 2, N_HEADS, HEAD_DIM, PAGE_SIZE))
    ck5 = cache_k.transpose(0, 2, 3, 1).reshape(cache_k.shape[0], N_HEADS, 2, HEAD_DIM, PAGE_SIZE)
    o_s = _attn_sample(page_table, qb, q2, kn, v_s.reshape(ns, N_HEADS, V_DIM), lamp, ck5, cache_v, g=8)
    y_s = _epi_sample(o_s.reshape(ns, D), sg_s, x1s, mod1[nb:], sub, wout1, fng)

    k_p = kt_p.transpose(0, 3, 1, 2)
    return (y_p, y_s.reshape(ns, 1, D), lru_h_p, hl_s, conv_a_p, conv_a_s, conv_b_p, conv_b_s,
            k_p, v_p.reshape(nb, t, N_HEADS, V_DIM), k_s.reshape(ns, 1, 2 * N_HEADS, HEAD_DIM),
            v_s.reshape(ns, 1, N_HEADS, V_DIM))
```

```python
import functools
import math

import jax
import jax.numpy as jnp
from jax import lax
from jax.experimental import pallas as pl
from jax.experimental.pallas import tpu as pltpu

F32 = jnp.float32
BF16 = jnp.bfloat16

D = 1024
LRU_HEADS = 16
LRU_BLK = D // LRU_HEADS
CONV_A = 4
CONV_B = 3
RG_C = 8.0
N_HEADS = 8
HEAD_DIM = 64
V_DIM = 128
ROT_DIM = 16
ROPE_THETA = 500000.0
LAMBDA_INIT = 0.8 - 0.6 * math.exp(-0.3 * 1)
SCALE = HEAD_DIM ** -0.5
LOG2E = math.log2(math.e)
PAGE_SIZE = 128
EPS = 1e-6
NEG = -1e30

N_SLICE = 4
SLICE_W = D // N_SLICE
CARRY_PAD = 8
VMEM_LIMIT = 56 * 1024 * 1024


def _silu(x):
    return x * jax.nn.sigmoid(x)


def _softplus(x):
    return jnp.maximum(x, 0.0) + jnp.log1p(jnp.exp(-jnp.abs(x)))


def _rms(x, g):
    return x * lax.rsqrt(jnp.mean(x * x, axis=-1, keepdims=True) + EPS) * g


def _dot(a, b):
    return jnp.dot(a, b, preferred_element_type=F32)


def _ada_kernel(c_ref, w_ref, b_ref, o_ref):
    s = _silu(c_ref[...]).astype(BF16)
    o_ref[...] = _dot(s, w_ref[...].astype(BF16)) + b_ref[...]


def _ada(c, w, b):
    n, tn = w.shape[1], 768
    return pl.pallas_call(
        _ada_kernel,
        grid=(n // tn,),
        in_specs=[pl.BlockSpec(c.shape, lambda j: (0, 0)),
                  pl.BlockSpec((D, tn), lambda j: (0, j)),
                  pl.BlockSpec((1, tn), lambda j: (0, j))],
        out_specs=pl.BlockSpec((c.shape[0], tn), lambda j: (0, j)),
        out_shape=jax.ShapeDtypeStruct((c.shape[0], n), F32),
        name="ada_mod",
    )(c, w, b.reshape(1, n))


def _lru_gates(xc, ri, br, bi, lam):
    w = xc.shape[1]
    r = jax.nn.sigmoid(ri[:, :w] + br)
    ig = jax.nn.sigmoid(ri[:, w:] + bi)
    log_a = (-RG_C * _softplus(-lam)) * r
    a = jnp.exp(log_a)
    u = jnp.sqrt(1.0 - jnp.exp(2.0 * log_a)) * (ig * xc)
    return a, u


def _l0_prompt_kernel(x_ref, mod_ref, ng_ref, win_ref, caw_ref, cab_ref, wg_ref, br_ref, bi_ref,
                      lam_ref, cbw_ref, wout_ref,
                      x1_ref, hl_ref, ca_ref, cb_ref,
                      xk_s, hbc_s, hbn_s, za_s, zb_s, oacc_s, exta_s, extb_s, xc_s, y_s, h_s, *, nb, tb):
    k = pl.program_id(0)
    pa = CARRY_PAD - (CONV_A - 1)
    pb = CARRY_PAD - (CONV_B - 1)
    w = SLICE_W

    def norm_incoming(dst):
        for b in range(nb):
            hn = _rms(x_ref[b], ng_ref[...])
            hn = hn * (1.0 + mod_ref[b:b + 1, D:2 * D]) + mod_ref[b:b + 1, 0:D]
            dst[b * tb:(b + 1) * tb, :] = hn.astype(BF16)

    @pl.when(k == 0)
    def _():
        exta_s[:, :, 0:CARRY_PAD, :] = jnp.zeros((N_SLICE, nb, CARRY_PAD, w), F32)
        extb_s[:, :, 0:CARRY_PAD, :] = jnp.zeros((N_SLICE, nb, CARRY_PAD, w), F32)
        h_s[...] = jnp.zeros(h_s.shape, F32)
        norm_incoming(hbn_s)
        za_s[...] = _dot(hbn_s[...], win_ref[0])
        xk_s[...] = x_ref[...]

    row8 = lax.broadcasted_iota(jnp.int32, (tb, w), 0) & 7

    def out_chunk(s):
        return _dot(y_s[s], wout_ref[s])

    def slice_stage(s, zc, zn):
        if s == 1:
            oacc_s[...] = out_chunk(0)
        elif s > 1:
            oacc_s[...] = oacc_s[...] + out_chunk(s - 1)
        caw = caw_ref[s]
        for b in range(nb):
            exta_s[s, b, CARRY_PAD:CARRY_PAD + tb, :] = zc[b * tb:(b + 1) * tb, 0:w]
            acc = caw[0:1] * exta_s[s, b, pl.ds(pa, tb), :]
            for j in range(1, CONV_A):
                acc = acc + caw[j:j + 1] * exta_s[s, b, pl.ds(pa + j, tb), :]
            xc_s[b * tb:(b + 1) * tb, :] = acc + cab_ref[s]
            tail = exta_s[s, b, pl.ds(pa + tb, CONV_A - 1), :]
            exta_s[s, b, pa:CARRY_PAD, :] = tail
            ca_ref[s, b] = tail
        xc = xc_s[...]
        ri = _dot(xc.astype(BF16), wg_ref[s])
        if s + 1 < N_SLICE:
            zn[...] = _dot(hbc_s[...], win_ref[s + 1])
        else:
            zn[...] = _dot(hbn_s[...], win_ref[0])
        a, u = _lru_gates(xc, ri, br_ref[s], bi_ref[s], lam_ref[s])
        for b in range(nb):
            rows = slice(b * tb, (b + 1) * tb)
            ab = a[rows]
            ub = u[rows]
            for d in (1, 2, 4):
                keep = row8 >= d
                a_sh = jnp.where(keep, pltpu.roll(ab, d, axis=0), 1.0)
                u_sh = jnp.where(keep, pltpu.roll(ub, d, axis=0), 0.0)
                ub = ab * u_sh + ub
                ab = ab * a_sh
            hprev = h_s[s, b:b + 1, :]
            hs = []
            for g in range(tb // 8):
                hg = ub[8 * g:8 * g + 8] + ab[8 * g:8 * g + 8] * hprev
                hprev = hg[7:8]
                hs.append(hg)
            h_s[s, b:b + 1, :] = hprev
            hl_ref[s, b:b + 1, :] = hprev
            ya = jnp.concatenate(hs, axis=0) * _silu(zc[rows, w:2 * w])
            y_s[s, rows, 0:w] = ya.astype(BF16)
        cbw = cbw_ref[s]
        for b in range(nb):
            rows = slice(b * tb, (b + 1) * tb)
            extb_s[s, b, CARRY_PAD:CARRY_PAD + tb, :] = zc[rows, 3 * w:4 * w] * zc[rows, 4 * w:5 * w]
            uc = cbw[0:1] * extb_s[s, b, pl.ds(pb, tb), :]
            for j in range(1, CONV_B):
                uc = uc + cbw[j:j + 1] * extb_s[s, b, pl.ds(pb + j, tb), :]
            tail = extb_s[s, b, pl.ds(pb + tb, CONV_B - 1), :]
            extb_s[s, b, pb:CARRY_PAD, :] = tail
            cb_ref[s, b] = tail
            yb = zc[rows, 2 * w:3 * w] * uc * _silu(zc[rows, 5 * w:6 * w])
            y_s[s, rows, w:2 * w] = yb.astype(BF16)

    @pl.when(k > 0)
    def _():
        hbc_s[...] = hbn_s[...]
        norm_incoming(hbn_s)
        for s in range(N_SLICE):
            zc, zn = (za_s, zb_s) if s % 2 == 0 else (zb_s, za_s)
            slice_stage(s, zc, zn)
        out = oacc_s[...] + out_chunk(N_SLICE - 1)
        for b in range(nb):
            x1_ref[b] = xk_s[b] + mod_ref[b:b + 1, 2 * D:3 * D] * out[b * tb:(b + 1) * tb]
        xk_s[...] = x_ref[...]


def _const_spec(shape):
    nd = len(shape)
    return pl.BlockSpec(shape, lambda *_: (0,) * nd, pipeline_mode=pl.Buffered(1))


def _l0_prompt(x, mod, ng, w):
    nb, t, _ = x.shape
    tb = 64
    m = nb * tb
    nt = t // tb
    kern = functools.partial(_l0_prompt_kernel, nb=nb, tb=tb)
    wshapes = [a.shape for a in w]
    return pl.pallas_call(
        kern,
        grid=(nt + 1,),
        in_specs=[pl.BlockSpec((nb, tb, D), lambda k: (0, jnp.minimum(k, nt - 1), 0)),
                  _const_spec(mod.shape), _const_spec(ng.shape)] + [_const_spec(s) for s in wshapes],
        out_specs=[pl.BlockSpec((nb, tb, D), lambda k: (0, jnp.maximum(k - 1, 0), 0)),
                   pl.BlockSpec((N_SLICE, nb, SLICE_W), lambda k: (0, 0, 0)),
                   pl.BlockSpec((N_SLICE, nb, CONV_A - 1, SLICE_W), lambda k: (0, 0, 0, 0)),
                   pl.BlockSpec((N_SLICE, nb, CONV_B - 1, SLICE_W), lambda k: (0, 0, 0, 0))],
        out_shape=[jax.ShapeDtypeStruct((nb, t, D), F32),
                   jax.ShapeDtypeStruct((N_SLICE, nb, SLICE_W), F32),
                   jax.ShapeDtypeStruct((N_SLICE, nb, CONV_A - 1, SLICE_W), F32),
                   jax.ShapeDtypeStruct((N_SLICE, nb, CONV_B - 1, SLICE_W), F32)],
        scratch_shapes=[pltpu.VMEM((nb, tb, D), F32),
                        pltpu.VMEM((m, D), BF16),
                        pltpu.VMEM((m, D), BF16),
                        pltpu.VMEM((m, 6 * SLICE_W), F32),
                        pltpu.VMEM((m, 6 * SLICE_W), F32),
                        pltpu.VMEM((m, D), F32),
                        pltpu.VMEM((N_SLICE, nb, CARRY_PAD + tb, SLICE_W), F32),
                        pltpu.VMEM((N_SLICE, nb, CARRY_PAD + tb, SLICE_W), F32),
                        pltpu.VMEM((m, SLICE_W), F32),
                        pltpu.VMEM((N_SLICE, m, 2 * SLICE_W), BF16),
                        pltpu.VMEM((N_SLICE, nb, SLICE_W), F32)],
        compiler_params=pltpu.CompilerParams(dimension_semantics=("arbitrary",),
                                             vmem_limit_bytes=VMEM_LIMIT),
        name="l0_prompt",
    )(x, mod, ng, *w)


def _l0_sample_kernel(x_ref, mod_ref, ng_ref, h0_ref, ca0_ref, cb0_ref, win_ref, caw_ref, cab_ref,
                      wg_ref, br_ref, bi_ref, lam_ref, cbw_ref, wout_ref,
                      x1_ref, hl_ref, ca_ref, cb_ref):
    x = x_ref[...]
    hn = _rms(x, ng_ref[...])
    hb = (hn * (1.0 + mod_ref[:, D:2 * D]) + mod_ref[:, 0:D]).astype(BF16)
    out = jnp.zeros(x.shape, F32)
    for s in range(N_SLICE):
        sl = slice(s * SLICE_W, (s + 1) * SLICE_W)
        z = _dot(hb, win_ref[s])
        xa = z[:, 0:SLICE_W]
        caw = caw_ref[s]
        xc = caw[CONV_A - 1:CONV_A] * xa + cab_ref[s]
        for j in range(CONV_A - 1):
            xc = xc + caw[j:j + 1] * ca0_ref[j, :, sl]
        for j in range(CONV_A - 2):
            ca_ref[j, :, sl] = ca0_ref[j + 1, :, sl]
        ca_ref[CONV_A - 2, :, sl] = xa
        ri = _dot(xc.astype(BF16), wg_ref[s])
        a, u = _lru_gates(xc, ri, br_ref[s], bi_ref[s], lam_ref[s])
        h = a * h0_ref[:, sl] + u
        hl_ref[:, sl] = h
        ya = h * _silu(z[:, SLICE_W:2 * SLICE_W])
        gate_b = z[:, 2 * SLICE_W:3 * SLICE_W]
        u2 = z[:, 3 * SLICE_W:4 * SLICE_W] * z[:, 4 * SLICE_W:5 * SLICE_W]
        gsil = _silu(z[:, 5 * SLICE_W:6 * SLICE_W])
        cbw = cbw_ref[s]
        uc = cbw[CONV_B - 1:CONV_B] * u2
        for j in range(CONV_B - 1):
            uc = uc + cbw[j:j + 1] * cb0_ref[j, :, sl]
        for j in range(CONV_B - 2):
            cb_ref[j, :, sl] = cb0_ref[j + 1, :, sl]
        cb_ref[CONV_B - 2, :, sl] = u2
        yb = gate_b * uc * gsil
        out = out + _dot(jnp.concatenate([ya, yb], axis=1).astype(BF16), wout_ref[s])
    x1_ref[...] = x + mod_ref[:, 2 * D:3 * D] * out


def _l0_sample(x, mod, ng, h0, ca0, cb0, w):
    n = x.shape[0]
    return pl.pallas_call(
        _l0_sample_kernel,
        out_shape=[jax.ShapeDtypeStruct((n, D), F32), jax.ShapeDtypeStruct((n, D), F32),
                   jax.ShapeDtypeStruct((CONV_A - 1, n, D), F32),
                   jax.ShapeDtypeStruct((CONV_B - 1, n, D), F32)],
        compiler_params=pltpu.CompilerParams(vmem_limit_bytes=VMEM_LIMIT),
        name="l0_sample",
    )(x, mod, ng, h0, ca0, cb0, *w)


def _rope_lanes(x, tab_ref):
    outs = []
    for j in range(D // 128):
        xj = x[:, 128 * j:128 * (j + 1)]
        outs.append(xj * tab_ref[0] + pltpu.roll(xj, 8, axis=1) * tab_ref[1]
                    + pltpu.roll(xj, 120, axis=1) * tab_ref[2])
    return outs


def _rope_rows(xt, c, s, scale):
    x3 = xt.reshape(2 * N_HEADS, HEAD_DIM, xt.shape[1])
    half = ROT_DIM // 2
    x1 = x3[:, 0:half, :]
    x2 = x3[:, half:ROT_DIM, :]
    rest = x3[:, ROT_DIM:, :]
    if scale != 1.0:
        rest = rest * scale
    return jnp.concatenate([x1 * c - x2 * s, x2 * c + x1 * s, rest], axis=1)


_NT = (((1,), (1,)), ((), ()))


def _qkvg_prompt_kernel(x_ref, mod_ref, ng_ref, wqt_ref, wkt_ref, wk_ref, wv_ref, wvt_ref, wgt_ref,
                        tk_ref, ck_ref, sk_ref,
                        qt_ref, kt_ref, kb_ref, v_ref, vtb_ref, sgt_ref, *, tb, tk):
    b = pl.program_id(0)
    hn = _rms(x_ref[0], ng_ref[...])
    hb = (hn * (1.0 + mod_ref[pl.ds(b, 1), D:2 * D]) + mod_ref[pl.ds(b, 1), 0:D]).astype(BF16)
    ck = ck_ref[...]
    sk = sk_ref[...]
    qs = SCALE * LOG2E
    qt = lax.dot_general(wqt_ref[...], hb, _NT, preferred_element_type=F32)
    qt_ref[0] = _rope_rows(qt, ck * qs, sk * qs, qs).reshape(D, tb).astype(BF16)
    kt = lax.dot_general(wkt_ref[...], hb, _NT, preferred_element_type=F32)
    kt_ref[0] = _rope_rows(kt, ck, sk, 1.0)
    for j, kj in enumerate(_rope_lanes(_dot(hb, wk_ref[...]), tk_ref)):
        kb_ref[0, :, 128 * j:128 * (j + 1)] = kj.astype(BF16)
    v_ref[0] = _dot(hb, wv_ref[...])
    vt = lax.dot_general(wvt_ref[...], hb, _NT, preferred_element_type=F32).astype(BF16)
    for j in range(tb // tk):
        vtb_ref[0, j] = vt[:, j * tk:(j + 1) * tk]
    sgt_ref[0] = _silu(lax.dot_general(wgt_ref[...], hb, _NT, preferred_element_type=F32)).astype(BF16)


def _qkvg_prompt(x1, mod, ng, wqt, wkt, wk, wv, wvt, wgt, tabk, ck, sk, *, tk):
    nb, t, _ = x1.shape
    tb = 512
    kern = functools.partial(_qkvg_prompt_kernel, tb=tb, tk=tk)
    row = pl.BlockSpec((1, tb, D), lambda b, i: (b, i, 0))
    col = pl.BlockSpec((1, D, tb), lambda b, i: (b, 0, i))
    wspec = _const_spec(wqt.shape)
    return pl.pallas_call(
        kern,
        grid=(nb, t // tb),
        in_specs=[row, _const_spec(mod.shape), _const_spec(ng.shape)] + [wspec] * 6 + [
            pl.BlockSpec((3, tb, 128), lambda b, i: (0, i, 0)),
            pl.BlockSpec((ROT_DIM // 2, tb), lambda b, i: (0, i)),
            pl.BlockSpec((ROT_DIM // 2, tb), lambda b, i: (0, i))],
        out_specs=[col,
                   pl.BlockSpec((1, 2 * N_HEADS, HEAD_DIM, tb), lambda b, i: (b, 0, 0, i)),
                   row, row,
                   pl.BlockSpec((1, tb // tk, D, tk), lambda b, i: (b, i, 0, 0)),
                   col],
        out_shape=[jax.ShapeDtypeStruct((nb, D, t), BF16),
                   jax.ShapeDtypeStruct((nb, 2 * N_HEADS, HEAD_DIM, t), F32),
                   jax.ShapeDtypeStruct((nb, t, D), BF16),
                   jax.ShapeDtypeStruct((nb, t, D), F32),
                   jax.ShapeDtypeStruct((nb, t // tk, D, tk), BF16),
                   jax.ShapeDtypeStruct((nb, D, t), BF16)],
        compiler_params=pltpu.CompilerParams(dimension_semantics=("arbitrary", "arbitrary"),
                                             vmem_limit_bytes=VMEM_LIMIT),
        name="qkvg_prompt",
    )(x1, mod, ng, wqt, wkt, wk, wv, wvt, wgt, tabk, ck, sk)


def _qkvg_sample_kernel(x_ref, mod_ref, ng_ref, wq_ref, wk_ref, wv_ref, wg_ref, tq_ref, tk_ref,
                        q_ref, k_ref, v_ref, sg_ref):
    hn = _rms(x_ref[...], ng_ref[...])
    hb = (hn * (1.0 + mod_ref[:, D:2 * D]) + mod_ref[:, 0:D]).astype(BF16)
    for j, qj in enumerate(_rope_lanes(_dot(hb, wq_ref[...]), tq_ref)):
        q_ref[:, 128 * j:128 * (j + 1)] = qj
    for j, kj in enumerate(_rope_lanes(_dot(hb, wk_ref[...]), tk_ref)):
        k_ref[:, 128 * j:128 * (j + 1)] = kj
    v_ref[...] = _dot(hb, wv_ref[...])
    sg_ref[...] = _silu(_dot(hb, wg_ref[...]))


def _qkvg_sample(x1, mod, ng, wq, wk, wv, wg, tabq, tabk):
    n = x1.shape[0]
    return pl.pallas_call(
        _qkvg_sample_kernel,
        out_shape=[jax.ShapeDtypeStruct((n, D), F32)] * 4,
        compiler_params=pltpu.CompilerParams(vmem_limit_bytes=VMEM_LIMIT),
        name="qkvg_sample",
    )(x1, mod, ng, wq, wk, wv, wg, tabq, tabk)


def _diff_lambda(lp):
    e1 = jnp.exp(jnp.sum(lp[0:1] * lp[1:2], axis=-1, keepdims=True))
    e2 = jnp.exp(jnp.sum(lp[2:3] * lp[3:4], axis=-1, keepdims=True))
    return e1 - e2 + LAMBDA_INIT


def _subln(o, sub):
    return _rms(o, sub) * (1.0 - LAMBDA_INIT)


PAGES_PER_ITEM = 8
ITEMS_PER_BLOCK = 2


def _attn_kernel(pt_ref,
                 qt_ref, kb_ref, vt_ref, sgt_ref, x1_ref, mod_ref, lamp_ref, sub_ref, wout_ref, fng_ref,
                 q2t_ref, q2_ref, kn_ref, vn_ref, rep_ref, ck_hbm, cv_hbm,
                 y_ref, os_ref,
                 o_s, qq_s, m_s, l_s, acc_s, st_s,
                 kbuf, vbuf, qb_s, ms_s, ls_s, accs_s, cnt_s, sem_k, sem_v,
                 *, tq, tk, n_items, steps_per_seq):
    b = pl.program_id(0)
    qi = pl.program_id(1)
    g = PAGES_PER_ITEM
    nr = 2 * N_HEADS

    def fetch(item, slot, start):
        sb = item // steps_per_seq
        sp = item % steps_per_seq
        for i in range(g):
            pid = pt_ref[sb, sp * g + i]
            for cp in (pltpu.make_async_copy(ck_hbm.at[pid], kbuf.at[slot, i], sem_k.at[slot]),
                       pltpu.make_async_copy(cv_hbm.at[pid], vbuf.at[slot, i], sem_v.at[slot])):
                if start:
                    cp.start()
                else:
                    cp.wait()

    def compute_item(item, slot):
        sb = item // steps_per_seq
        sp = item % steps_per_seq

        @pl.when(sp == 0)
        def _():
            qcol = q2t_ref[sb]
            for n in range(nr):
                qb_s[n // N_HEADS, n % N_HEADS] = jnp.broadcast_to(qcol[:, n:n + 1], (HEAD_DIM, PAGE_SIZE))
            ms_s[...] = jnp.sum(q2_ref[sb] * kn_ref[sb], axis=-1, keepdims=True)
            ls_s[...] = jnp.ones((nr, 1), F32)
            accs_s[...] = jnp.concatenate([vn_ref[sb], vn_ref[sb]], axis=0)

        rows = []
        for i in range(g):
            for c in range(2):
                rows.append(jnp.sum(kbuf[slot, i, :, c] * qb_s[c], axis=1))
        s = jnp.concatenate(rows, axis=0)
        m_blk = jnp.max(jnp.max(s, axis=-1, keepdims=True).reshape(g, nr, 1), axis=0)
        m_old = ms_s[...]
        m_new = jnp.maximum(m_old, m_blk)
        corr = jnp.exp(m_old - m_new)
        pr = jnp.exp(s - jnp.tile(m_new, (g, 1)))
        ls_s[...] = ls_s[...] * corr + jnp.sum(jnp.sum(pr, axis=-1, keepdims=True).reshape(g, nr, 1), axis=0)
        ms_s[...] = m_new
        pexp = _dot(pr.astype(BF16), rep_ref[...])
        lane = lax.broadcasted_iota(jnp.int32, pexp.shape, 1)
        row = lax.broadcasted_iota(jnp.int32, pexp.shape, 0)
        pexp = jnp.where((lane & (N_HEADS - 1)) == (row & (N_HEADS - 1)), pexp, 0.0).astype(BF16)
        acc = accs_s[...] * corr
        for i in range(g):
            v2 = vbuf[slot, i].reshape(PAGE_SIZE * N_HEADS, V_DIM).astype(BF16)
            acc = acc + _dot(pexp[nr * i:nr * (i + 1)], v2)
        accs_s[...] = acc

        @pl.when(sp == steps_per_seq - 1)
        def _():
            o12 = accs_s[...] / ls_s[...]
            os_ref[sb] = o12[:N_HEADS] - _diff_lambda(lamp_ref[...]) * o12[N_HEADS:]

    def item_slot():
        item = cnt_s[0]

        @pl.when(item < n_items)
        def _():
            slot = item & 1

            @pl.when(item + 1 < n_items)
            def _():
                fetch(item + 1, 1 - slot, True)

            fetch(item, slot, False)
            compute_item(item, slot)
            cnt_s[0] = item + 1

    @pl.when((b == 0) & (qi == 0))
    def _():
        cnt_s[0] = 0
        fetch(0, 0, True)

    row = lax.broadcasted_iota(jnp.int32, (2 * HEAD_DIM, tq), 0)
    for h in range(N_HEADS):
        qh = qt_ref[0, V_DIM * h:V_DIM * (h + 1), :]
        zero = jnp.zeros_like(qh)
        qq_s[h, :, 0:tq] = jnp.where(row < HEAD_DIM, qh, zero)
        qq_s[h, :, tq:2 * tq] = jnp.where(row >= HEAD_DIM, qh, zero)
    m_s[...] = jnp.full(m_s.shape, NEG, F32)
    l_s[...] = jnp.zeros(l_s.shape, F32)
    acc_s[...] = jnp.zeros(acc_s.shape, F32)

    def scores(j, h):
        hs = slice(V_DIM * h, V_DIM * (h + 1))
        return _dot(kb_ref[0, pl.ds(pl.multiple_of(j * tk, tk), tk), hs], qq_s[h])

    def update(j, h, st, masked):
        if masked:
            keyi = lax.broadcasted_iota(jnp.int32, (tk, 2 * tq), 0)
            qryi = lax.broadcasted_iota(jnp.int32, (tk, 2 * tq), 1) & (tq - 1)
            st = jnp.where(keyi <= qryi, st, NEG)
        m_old = m_s[h]
        m_new = jnp.maximum(m_old, jnp.max(st, axis=0, keepdims=True))
        corr = jnp.exp2(m_old - m_new)
        p = jnp.exp2(st - m_new)
        m_s[h] = m_new
        l_s[h] = l_s[h] * corr + jnp.sum(p, axis=0, keepdims=True)
        acc_s[h] = acc_s[h] * corr + _dot(vt_ref[0, j, V_DIM * h:V_DIM * (h + 1), :], p.astype(BF16))

    for h in range(N_HEADS):
        st_s[h] = scores(0, h)

    def body(j, carry):
        for _ in range(ITEMS_PER_BLOCK):
            item_slot()
        for h in range(N_HEADS):
            cur = st_s[h]
            nxt = scores(j + 1, h)
            update(j, h, cur, False)
            st_s[h] = nxt
        return carry

    lax.fori_loop(0, qi, body, 0)
    for _ in range(ITEMS_PER_BLOCK):
        item_slot()
    for h in range(N_HEADS):
        update(qi, h, st_s[h], True)

    lam = _diff_lambda(lamp_ref[...])
    for h in range(N_HEADS):
        hs = slice(V_DIM * h, V_DIM * (h + 1))
        o12 = acc_s[h] / l_s[h]
        ot = o12[:, :tq] - lam * o12[:, tq:]
        ms = jnp.mean(ot * ot, axis=0, keepdims=True)
        on = ot * lax.rsqrt(ms + EPS) * sub_ref[...] * (1.0 - LAMBDA_INIT)
        o_s[hs, :] = (on * sgt_ref[0, hs, :].astype(F32)).astype(BF16)
    out = lax.dot_general(o_s[...], wout_ref[...], (((0,), (0,)), ((), ())), preferred_element_type=F32)
    x2 = x1_ref[0] + mod_ref[pl.ds(b, 1), 2 * D:3 * D] * out
    y_ref[0] = _rms(x2, fng_ref[...])


def _attn(qt, kb, vtb, sgt, x1, mod, lamp, sub, wout, fng, page_table, q2, kn, vn, ck5, cv, *, tq):
    nb, t, _ = kb.shape
    tk = vtb.shape[-1]
    ns, n_pages = page_table.shape
    g = PAGES_PER_ITEM
    steps_per_seq = n_pages // g
    n_items = ns * steps_per_seq
    n_blocks = nb * sum(range(1, t // tq + 1))
    assert n_pages % g == 0 and n_items <= ITEMS_PER_BLOCK * n_blocks
    nr = 2 * N_HEADS
    rep = (jnp.arange(PAGE_SIZE * N_HEADS)[None, :] // N_HEADS == jnp.arange(PAGE_SIZE)[:, None]).astype(BF16)
    q2 = q2.reshape(ns, nr, HEAD_DIM)
    kn = kn.reshape(ns, nr, HEAD_DIM)
    kern = functools.partial(_attn_kernel, tq=tq, tk=tk, n_items=n_items, steps_per_seq=steps_per_seq)
    row = pl.BlockSpec((1, tq, D), lambda b, i, pt: (b, i, 0))
    col = pl.BlockSpec((1, D, tq), lambda b, i, pt: (b, 0, i))

    def const(shape):
        nd = len(shape)
        return pl.BlockSpec(shape, lambda b, i, pt: (0,) * nd, pipeline_mode=pl.Buffered(1))

    grid_spec = pltpu.PrefetchScalarGridSpec(
        num_scalar_prefetch=1,
        grid=(nb, t // tq),
        in_specs=[col,
                  pl.BlockSpec((1, t, D), lambda b, i, pt: (b, 0, 0)),
                  pl.BlockSpec((1, t // tk, D, tk), lambda b, i, pt: (b, 0, 0, 0)),
                  col, row, const(mod.shape), const(lamp.shape), const(sub.shape),
                  const(wout.shape), const(fng.shape),
                  const((ns, HEAD_DIM, nr)), const(q2.shape), const(kn.shape), const(vn.shape), const(rep.shape),
                  pl.BlockSpec(memory_space=pl.ANY), pl.BlockSpec(memory_space=pl.ANY)],
        out_specs=[row, pl.BlockSpec((ns, N_HEADS, V_DIM), lambda b, i, pt: (0, 0, 0))],
        scratch_shapes=[pltpu.VMEM((D, tq), BF16),
                        pltpu.VMEM((N_HEADS, 2 * HEAD_DIM, 2 * tq), BF16),
                        pltpu.VMEM((N_HEADS, 1, 2 * tq), F32),
                        pltpu.VMEM((N_HEADS, 1, 2 * tq), F32),
                        pltpu.VMEM((N_HEADS, V_DIM, 2 * tq), F32),
                        pltpu.VMEM((N_HEADS, tk, 2 * tq), F32),
                        pltpu.VMEM((2, g, N_HEADS, 2, HEAD_DIM, PAGE_SIZE), F32),
                        pltpu.VMEM((2, g, PAGE_SIZE, N_HEADS, V_DIM), F32),
                        pltpu.VMEM((2, N_HEADS, HEAD_DIM, PAGE_SIZE), F32),
                        pltpu.VMEM((nr, 1), F32), pltpu.VMEM((nr, 1), F32), pltpu.VMEM((nr, V_DIM), F32),
                        pltpu.SMEM((1,), jnp.int32),
                        pltpu.SemaphoreType.DMA((2,)), pltpu.SemaphoreType.DMA((2,))],
    )
    return pl.pallas_call(
        kern,
        grid_spec=grid_spec,
        out_shape=[jax.ShapeDtypeStruct((nb, t, D), F32), jax.ShapeDtypeStruct((ns, N_HEADS, V_DIM), F32)],
        compiler_params=pltpu.CompilerParams(dimension_semantics=("arbitrary", "arbitrary"),
                                             vmem_limit_bytes=VMEM_LIMIT),
        name="attn",
    )(page_table, qt, kb, vtb, sgt, x1, mod, lamp, sub, wout, fng,
      q2.transpose(0, 2, 1), q2, kn, vn, rep, ck5, cv)


def _epi_sample_kernel(o_ref, sg_ref, x1_ref, mod_ref, sub_ref, wout_ref, fng_ref, y_ref):
    parts = []
    for h in range(N_HEADS):
        hs = slice(V_DIM * h, V_DIM * (h + 1))
        parts.append((_subln(o_ref[:, hs], sub_ref[...]) * sg_ref[:, hs]).astype(BF16))
    out = _dot(jnp.concatenate(parts, axis=1), wout_ref[...])
    x2 = x1_ref[...] + mod_ref[:, 2 * D:3 * D] * out
    y_ref[...] = _rms(x2, fng_ref[...])


def _epi_sample(o, sg, x1, mod, sub, wout, fng):
    return pl.pallas_call(
        _epi_sample_kernel,
        out_shape=jax.ShapeDtypeStruct(x1.shape, F32),
        name="epi_sample",
    )(o, sg, x1, mod, sub, wout, fng)


def _block_diag_slices(w):
    per = LRU_HEADS // N_SLICE
    w4 = w.reshape(N_SLICE, per, LRU_BLK, LRU_BLK)
    eye = jnp.eye(per, dtype=w.dtype)
    return jnp.einsum('skcd,kl->skcld', w4, eye).reshape(N_SLICE, SLICE_W, SLICE_W)


def _sliced(p):
    return p.reshape(p.shape[0], N_SLICE, SLICE_W).transpose(1, 0, 2)


def _l0_weights(w_in, conv_a_w, conv_a_b, wr, br, wi, bi, lam, conv_b_w, w_out):
    win = w_in.astype(BF16).reshape(D, 6, N_SLICE, SLICE_W).transpose(2, 0, 1, 3).reshape(N_SLICE, D, 6 * SLICE_W)
    wg = jnp.concatenate([_block_diag_slices(wr), _block_diag_slices(wi)], axis=-1).astype(BF16)
    wout = w_out.astype(BF16).reshape(2, N_SLICE, SLICE_W, D).transpose(1, 0, 2, 3).reshape(N_SLICE, 2 * SLICE_W, D)
    return (win, _sliced(conv_a_w), _sliced(conv_a_b[None]), wg, _sliced(br[None]), _sliced(bi[None]),
            _sliced(lam[None]), _sliced(conv_b_w), wout)


def _rope_lane_tables(pos, scale):
    half = ROT_DIM // 2
    inv = jnp.power(jnp.float32(ROPE_THETA), -jnp.arange(0, ROT_DIM, 2, dtype=F32) / ROT_DIM)
    ang = pos.astype(F32)[:, None] * inv[None, :]
    lane = jnp.arange(128)
    within = lane % HEAD_DIM
    cos = jnp.cos(ang)[:, lane % half]
    sin = jnp.sin(ang)[:, lane % half]
    c = jnp.where(within < ROT_DIM, cos, 1.0)
    sa = jnp.where((within >= half) & (within < ROT_DIM), sin, 0.0)
    sb = jnp.where(within < half, -sin, 0.0)
    return jnp.stack([c, sa, sb]) * scale


def kernel(x_prompt, x_sample, state_lru_h, state_conv_a, state_conv_b, cache_k, cache_v, page_table,
           c_prompt, c_sample, l0_norm_g, l0_ada_w, l0_ada_b, l0_w_in, l0_conv_a_w, l0_conv_a_b,
           l0_lru_wr, l0_lru_br, l0_lru_wi, l0_lru_bi, l0_lru_lam, l0_conv_b_w, l0_w_out,
           l1_norm_g, l1_ada_w, l1_ada_b, l1_w_in, l1_lam_q1, l1_lam_k1, l1_lam_q2, l1_lam_k2,
           l1_subln_g, l1_w_out, final_norm_g):
    nb, t, _ = x_prompt.shape
    ns = x_sample.shape[0]
    n_pages = page_table.shape[1]
    past_len = n_pages * PAGE_SIZE
    tq = 256

    c_all = jnp.concatenate([c_prompt, c_sample], axis=0)
    mod0 = _ada(c_all, l0_ada_w, l0_ada_b)
    mod1 = _ada(c_all, l1_ada_w, l1_ada_b)

    w0 = _l0_weights(l0_w_in, l0_conv_a_w, l0_conv_a_b, l0_lru_wr, l0_lru_br, l0_lru_wi, l0_lru_bi,
                     l0_lru_lam, l0_conv_b_w, l0_w_out)
    ng0 = l0_norm_g.reshape(1, D)
    x1p, hl_p, ca_p, cb_p = _l0_prompt(x_prompt, mod0[:nb], ng0, w0)
    x1s, hl_s, ca_s, cb_s = _l0_sample(x_sample.reshape(ns, D), mod0[nb:], ng0, state_lru_h,
                                       state_conv_a.transpose(1, 0, 2), state_conv_b.transpose(1, 0, 2), w0)
    lru_h_p = hl_p.transpose(1, 0, 2).reshape(nb, D)
    conv_a_p = ca_p.transpose(1, 2, 0, 3).reshape(nb, CONV_A - 1, D)
    conv_b_p = cb_p.transpose(1, 2, 0, 3).reshape(nb, CONV_B - 1, D)
    conv_a_s = ca_s.transpose(1, 0, 2)
    conv_b_s = cb_s.transpose(1, 0, 2)

    w1 = l1_w_in.astype(BF16)
    wq, wk, wv, wg = w1[:, 0:D], w1[:, D:2 * D], w1[:, 2 * D:3 * D], w1[:, 3 * D:4 * D]
    ng1 = l1_norm_g.reshape(1, D)
    pos_p = jnp.arange(t, dtype=jnp.int32)
    pos_s = jnp.full((1,), past_len, jnp.int32)
    half = ROT_DIM // 2
    inv = jnp.power(jnp.float32(ROPE_THETA), -jnp.arange(0, ROT_DIM, 2, dtype=F32) / ROT_DIM)
    ang_t = inv[:, None] * pos_p.astype(F32)[None, :]
    qt_p, kt_p, kb_p, v_p, vtb_p, sgt_p = _qkvg_prompt(
        x1p, mod1[:nb], ng1, wq.T, wk.T, wk, wv, wv.T, wg.T, _rope_lane_tables(pos_p, 1.0),
        jnp.cos(ang_t), jnp.sin(ang_t), tk=tq)
    q_s, k_s, v_s, sg_s = _qkvg_sample(x1s, mod1[nb:], ng1, wq, wk, wv, wg,
                                       _rope_lane_tables(pos_s, SCALE), _rope_lane_tables(pos_s, 1.0))

    lamp = jnp.stack([l1_lam_q1, l1_lam_k1, l1_lam_q2, l1_lam_k2])
    sub = l1_subln_g.reshape(1, V_DIM)
    wout1 = l1_w_out.astype(BF16)
    fng = final_norm_g.reshape(1, D)
    q2 = q_s.reshape(ns, N_HEADS, 2, HEAD_DIM).transpose(0, 2, 1, 3)
    kn = k_s.reshape(ns, N_HEADS, 2, HEAD_DIM).transpose(0, 2, 1, 3)
    ck5 = cache_k.transpose(0, 2, 3, 1).reshape(cache_k.shape[0], N_HEADS, 2, HEAD_DIM, PAGE_SIZE)
    y_p, o_s = _attn(qt_p, kb_p, vtb_p, sgt_p, x1p, mod1[:nb], lamp, l1_subln_g.reshape(V_DIM, 1), wout1, fng,
                     page_table, q2, kn, v_s.reshape(ns, N_HEADS, V_DIM), ck5, cache_v, tq=tq)
    y_s = _epi_sample(o_s.reshape(ns, D), sg_s, x1s, mod1[nb:], sub, wout1, fng)

    k_p = kt_p.transpose(0, 3, 1, 2)
    return (y_p, y_s.reshape(ns, 1, D), lru_h_p, hl_s, conv_a_p, conv_a_s, conv_b_p, conv_b_s,
            k_p, v_p.reshape(nb, t, N_HEADS, V_DIM), k_s.reshape(ns, 1, 2 * N_HEADS, HEAD_DIM),
            v_s.reshape(ns, 1, N_HEADS, V_DIM))
```

```python
import functools
import math

import jax
import jax.numpy as jnp
from jax import lax
from jax.experimental import pallas as pl
from jax.experimental.pallas import tpu as pltpu

F32 = jnp.float32
BF16 = jnp.bfloat16

D = 1024
LRU_HEADS = 16
LRU_BLK = D // LRU_HEADS
CONV_A = 4
CONV_B = 3
RG_C = 8.0
N_HEADS = 8
HEAD_DIM = 64
V_DIM = 128
ROT_DIM = 16
ROPE_THETA = 500000.0
LAMBDA_INIT = 0.8 - 0.6 * math.exp(-0.3 * 1)
SCALE = HEAD_DIM ** -0.5
LOG2E = math.log2(math.e)
PAGE_SIZE = 128
EPS = 1e-6
NEG = -1e30

N_SLICE = 4
SLICE_W = D // N_SLICE
CARRY_PAD = 8
VMEM_LIMIT = 56 * 1024 * 1024


def _silu(x):
    return x * jax.nn.sigmoid(x)


def _softplus(x):
    return jnp.maximum(x, 0.0) + jnp.log1p(jnp.exp(-jnp.abs(x)))


def _rms(x, g):
    return x * lax.rsqrt(jnp.mean(x * x, axis=-1, keepdims=True) + EPS) * g


def _dot(a, b):
    return jnp.dot(a, b, preferred_element_type=F32)


def _ada_kernel(c_ref, w_ref, b_ref, o_ref):
    s = _silu(c_ref[...]).astype(BF16)
    o_ref[...] = _dot(s, w_ref[...].astype(BF16)) + b_ref[...]


def _ada(c, w, b):
    n, tn = w.shape[1], 768
    return pl.pallas_call(
        _ada_kernel,
        grid=(n // tn,),
        in_specs=[pl.BlockSpec(c.shape, lambda j: (0, 0)),
                  pl.BlockSpec((D, tn), lambda j: (0, j)),
                  pl.BlockSpec((1, tn), lambda j: (0, j))],
        out_specs=pl.BlockSpec((c.shape[0], tn), lambda j: (0, j)),
        out_shape=jax.ShapeDtypeStruct((c.shape[0], n), F32),
        name="ada_mod",
    )(c, w, b.reshape(1, n))


def _lru_gates(xc, ri, br, bi, lam):
    w = xc.shape[1]
    r = jax.nn.sigmoid(ri[:, :w] + br)
    ig = jax.nn.sigmoid(ri[:, w:] + bi)
    log_a = (-RG_C * _softplus(-lam)) * r
    a = jnp.exp(log_a)
    u = jnp.sqrt(1.0 - jnp.exp(2.0 * log_a)) * (ig * xc)
    return a, u


def _l0_prompt_kernel(x_ref, mod_ref, ng_ref, win_ref, caw_ref, cab_ref, wg_ref, br_ref, bi_ref,
                      lam_ref, cbw_ref, wout_ref,
                      x1_ref, hl_ref, ca_ref, cb_ref,
                      xk_s, hbc_s, hbn_s, za_s, zb_s, oacc_s, exta_s, extb_s, xc_s, y_s, h_s, *, nb, tb):
    k = pl.program_id(0)
    pa = CARRY_PAD - (CONV_A - 1)
    pb = CARRY_PAD - (CONV_B - 1)
    w = SLICE_W

    def norm_incoming(dst):
        for b in range(nb):
            hn = _rms(x_ref[b], ng_ref[...])
            hn = hn * (1.0 + mod_ref[b:b + 1, D:2 * D]) + mod_ref[b:b + 1, 0:D]
            dst[b * tb:(b + 1) * tb, :] = hn.astype(BF16)

    @pl.when(k == 0)
    def _():
        exta_s[:, :, 0:CARRY_PAD, :] = jnp.zeros((N_SLICE, nb, CARRY_PAD, w), F32)
        extb_s[:, :, 0:CARRY_PAD, :] = jnp.zeros((N_SLICE, nb, CARRY_PAD, w), F32)
        h_s[...] = jnp.zeros(h_s.shape, F32)
        norm_incoming(hbn_s)
        za_s[...] = _dot(hbn_s[...], win_ref[0])
        xk_s[...] = x_ref[...]

    row8 = lax.broadcasted_iota(jnp.int32, (tb, w), 0) & 7

    def out_chunk(s):
        return _dot(y_s[s], wout_ref[s])

    def slice_stage(s, zc, zn):
        if s == 1:
            oacc_s[...] = out_chunk(0)
        elif s > 1:
            oacc_s[...] = oacc_s[...] + out_chunk(s - 1)
        caw = caw_ref[s]
        for b in range(nb):
            exta_s[s, b, CARRY_PAD:CARRY_PAD + tb, :] = zc[b * tb:(b + 1) * tb, 0:w]
            acc = caw[0:1] * exta_s[s, b, pl.ds(pa, tb), :]
            for j in range(1, CONV_A):
                acc = acc + caw[j:j + 1] * exta_s[s, b, pl.ds(pa + j, tb), :]
            xc_s[b * tb:(b + 1) * tb, :] = acc + cab_ref[s]
            tail = exta_s[s, b, pl.ds(pa + tb, CONV_A - 1), :]
            exta_s[s, b, pa:CARRY_PAD, :] = tail
            ca_ref[s, b] = tail
        xc = xc_s[...]
        ri = _dot(xc.astype(BF16), wg_ref[s])
        if s + 1 < N_SLICE:
            zn[...] = _dot(hbc_s[...], win_ref[s + 1])
        else:
            zn[...] = _dot(hbn_s[...], win_ref[0])
        a, u = _lru_gates(xc, ri, br_ref[s], bi_ref[s], lam_ref[s])
        for b in range(nb):
            rows = slice(b * tb, (b + 1) * tb)
            ab = a[rows]
            ub = u[rows]
            for d in (1, 2, 4):
                keep = row8 >= d
                a_sh = jnp.where(keep, pltpu.roll(ab, d, axis=0), 1.0)
                u_sh = jnp.where(keep, pltpu.roll(ub, d, axis=0), 0.0)
                ub = ab * u_sh + ub
                ab = ab * a_sh
            hprev = h_s[s, b:b + 1, :]
            hs = []
            for g in range(tb // 8):
                hg = ub[8 * g:8 * g + 8] + ab[8 * g:8 * g + 8] * hprev
                hprev = hg[7:8]
                hs.append(hg)
            h_s[s, b:b + 1, :] = hprev
            hl_ref[s, b:b + 1, :] = hprev
            ya = jnp.concatenate(hs, axis=0) * _silu(zc[rows, w:2 * w])
            y_s[s, rows, 0:w] = ya.astype(BF16)
        cbw = cbw_ref[s]
        for b in range(nb):
            rows = slice(b * tb, (b + 1) * tb)
            extb_s[s, b, CARRY_PAD:CARRY_PAD + tb, :] = zc[rows, 3 * w:4 * w] * zc[rows, 4 * w:5 * w]
            uc = cbw[0:1] * extb_s[s, b, pl.ds(pb, tb), :]
            for j in range(1, CONV_B):
                uc = uc + cbw[j:j + 1] * extb_s[s, b, pl.ds(pb + j, tb), :]
            tail = extb_s[s, b, pl.ds(pb + tb, CONV_B - 1), :]
            extb_s[s, b, pb:CARRY_PAD, :] = tail
            cb_ref[s, b] = tail
            yb = zc[rows, 2 * w:3 * w] * uc * _silu(zc[rows, 5 * w:6 * w])
            y_s[s, rows, w:2 * w] = yb.astype(BF16)

    @pl.when(k > 0)
    def _():
        hbc_s[...] = hbn_s[...]
        norm_incoming(hbn_s)
        for s in range(N_SLICE):
            zc, zn = (za_s, zb_s) if s % 2 == 0 else (zb_s, za_s)
            slice_stage(s, zc, zn)
        out = oacc_s[...] + out_chunk(N_SLICE - 1)
        for b in range(nb):
            x1_ref[b] = xk_s[b] + mod_ref[b:b + 1, 2 * D:3 * D] * out[b * tb:(b + 1) * tb]
        xk_s[...] = x_ref[...]


def _const_spec(shape):
    nd = len(shape)
    return pl.BlockSpec(shape, lambda *_: (0,) * nd, pipeline_mode=pl.Buffered(1))


def _l0_prompt(x, mod, ng, w):
    nb, t, _ = x.shape
    tb = 64
    m = nb * tb
    nt = t // tb
    kern = functools.partial(_l0_prompt_kernel, nb=nb, tb=tb)
    wshapes = [a.shape for a in w]
    return pl.pallas_call(
        kern,
        grid=(nt + 1,),
        in_specs=[pl.BlockSpec((nb, tb, D), lambda k: (0, jnp.minimum(k, nt - 1), 0)),
                  _const_spec(mod.shape), _const_spec(ng.shape)] + [_const_spec(s) for s in wshapes],
        out_specs=[pl.BlockSpec((nb, tb, D), lambda k: (0, jnp.maximum(k - 1, 0), 0)),
                   pl.BlockSpec((N_SLICE, nb, SLICE_W), lambda k: (0, 0, 0)),
                   pl.BlockSpec((N_SLICE, nb, CONV_A - 1, SLICE_W), lambda k: (0, 0, 0, 0)),
                   pl.BlockSpec((N_SLICE, nb, CONV_B - 1, SLICE_W), lambda k: (0, 0, 0, 0))],
        out_shape=[jax.ShapeDtypeStruct((nb, t, D), F32),
                   jax.ShapeDtypeStruct((N_SLICE, nb, SLICE_W), F32),
                   jax.ShapeDtypeStruct((N_SLICE, nb, CONV_A - 1, SLICE_W), F32),
                   jax.ShapeDtypeStruct((N_SLICE, nb, CONV_B - 1, SLICE_W), F32)],
        scratch_shapes=[pltpu.VMEM((nb, tb, D), F32),
                        pltpu.VMEM((m, D), BF16),
                        pltpu.VMEM((m, D), BF16),
                        pltpu.VMEM((m, 6 * SLICE_W), F32),
                        pltpu.VMEM((m, 6 * SLICE_W), F32),
                        pltpu.VMEM((m, D), F32),
                        pltpu.VMEM((N_SLICE, nb, CARRY_PAD + tb, SLICE_W), F32),
                        pltpu.VMEM((N_SLICE, nb, CARRY_PAD + tb, SLICE_W), F32),
                        pltpu.VMEM((m, SLICE_W), F32),
                        pltpu.VMEM((N_SLICE, m, 2 * SLICE_W), BF16),
                        pltpu.VMEM((N_SLICE, nb, SLICE_W), F32)],
        compiler_params=pltpu.CompilerParams(dimension_semantics=("arbitrary",),
                                             vmem_limit_bytes=VMEM_LIMIT),
        name="l0_prompt",
    )(x, mod, ng, *w)


def _l0_sample_kernel(x_ref, mod_ref, ng_ref, h0_ref, ca0_ref, cb0_ref, win_ref, caw_ref, cab_ref,
                      wg_ref, br_ref, bi_ref, lam_ref, cbw_ref, wout_ref,
                      x1_ref, hl_ref, ca_ref, cb_ref):
    x = x_ref[...]
    hn = _rms(x, ng_ref[...])
    hb = (hn * (1.0 + mod_ref[:, D:2 * D]) + mod_ref[:, 0:D]).astype(BF16)
    out = jnp.zeros(x.shape, F32)
    for s in range(N_SLICE):
        sl = slice(s * SLICE_W, (s + 1) * SLICE_W)
        z = _dot(hb, win_ref[s])
        xa = z[:, 0:SLICE_W]
        caw = caw_ref[s]
        xc = caw[CONV_A - 1:CONV_A] * xa + cab_ref[s]
        for j in range(CONV_A - 1):
            xc = xc + caw[j:j + 1] * ca0_ref[j, :, sl]
        for j in range(CONV_A - 2):
            ca_ref[j, :, sl] = ca0_ref[j + 1, :, sl]
        ca_ref[CONV_A - 2, :, sl] = xa
        ri = _dot(xc.astype(BF16), wg_ref[s])
        a, u = _lru_gates(xc, ri, br_ref[s], bi_ref[s], lam_ref[s])
        h = a * h0_ref[:, sl] + u
        hl_ref[:, sl] = h
        ya = h * _silu(z[:, SLICE_W:2 * SLICE_W])
        gate_b = z[:, 2 * SLICE_W:3 * SLICE_W]
        u2 = z[:, 3 * SLICE_W:4 * SLICE_W] * z[:, 4 * SLICE_W:5 * SLICE_W]
        gsil = _silu(z[:, 5 * SLICE_W:6 * SLICE_W])
        cbw = cbw_ref[s]
        uc = cbw[CONV_B - 1:CONV_B] * u2
        for j in range(CONV_B - 1):
            uc = uc + cbw[j:j + 1] * cb0_ref[j, :, sl]
        for j in range(CONV_B - 2):
            cb_ref[j, :, sl] = cb0_ref[j + 1, :, sl]
        cb_ref[CONV_B - 2, :, sl] = u2
        yb = gate_b * uc * gsil
        out = out + _dot(jnp.concatenate([ya, yb], axis=1).astype(BF16), wout_ref[s])
    x1_ref[...] = x + mod_ref[:, 2 * D:3 * D] * out


def _l0_sample(x, mod, ng, h0, ca0, cb0, w):
    n = x.shape[0]
    return pl.pallas_call(
        _l0_sample_kernel,
        out_shape=[jax.ShapeDtypeStruct((n, D), F32), jax.ShapeDtypeStruct((n, D), F32),
                   jax.ShapeDtypeStruct((CONV_A - 1, n, D), F32),
                   jax.ShapeDtypeStruct((CONV_B - 1, n, D), F32)],
        compiler_params=pltpu.CompilerParams(vmem_limit_bytes=VMEM_LIMIT),
        name="l0_sample",
    )(x, mod, ng, h0, ca0, cb0, *w)


def _rope_lanes(x, tab_ref):
    outs = []
    for j in range(D // 128):
        xj = x[:, 128 * j:128 * (j + 1)]
        outs.append(xj * tab_ref[0] + pltpu.roll(xj, 8, axis=1) * tab_ref[1]
                    + pltpu.roll(xj, 120, axis=1) * tab_ref[2])
    return outs


def _rope_rows(xt, c, s, scale):
    x3 = xt.reshape(2 * N_HEADS, HEAD_DIM, xt.shape[1])
    half = ROT_DIM // 2
    x1 = x3[:, 0:half, :]
    x2 = x3[:, half:ROT_DIM, :]
    rest = x3[:, ROT_DIM:, :]
    if scale != 1.0:
        rest = rest * scale
    return jnp.concatenate([x1 * c - x2 * s, x2 * c + x1 * s, rest], axis=1)


_NT = (((1,), (1,)), ((), ()))


def _qkvg_prompt_kernel(x_ref, mod_ref, ng_ref, wqt_ref, wkt_ref, wk_ref, wv_ref, wvt_ref, wgt_ref,
                        tk_ref, ck_ref, sk_ref,
                        qt_ref, kt_ref, kb_ref, v_ref, vtb_ref, sgt_ref, *, tb, tk):
    b = pl.program_id(0)
    hn = _rms(x_ref[0], ng_ref[...])
    hb = (hn * (1.0 + mod_ref[pl.ds(b, 1), D:2 * D]) + mod_ref[pl.ds(b, 1), 0:D]).astype(BF16)
    ck = ck_ref[...]
    sk = sk_ref[...]
    qs = SCALE * LOG2E
    qt = lax.dot_general(wqt_ref[...], hb, _NT, preferred_element_type=F32)
    qt_ref[0] = _rope_rows(qt, ck * qs, sk * qs, qs).reshape(D, tb).astype(BF16)
    kt = lax.dot_general(wkt_ref[...], hb, _NT, preferred_element_type=F32)
    kt_ref[0] = _rope_rows(kt, ck, sk, 1.0)
    for j, kj in enumerate(_rope_lanes(_dot(hb, wk_ref[...]), tk_ref)):
        kb_ref[0, :, 128 * j:128 * (j + 1)] = kj.astype(BF16)
    v_ref[0] = _dot(hb, wv_ref[...])
    vt = lax.dot_general(wvt_ref[...], hb, _NT, preferred_element_type=F32).astype(BF16)
    for j in range(tb // tk):
        vtb_ref[0, j] = vt[:, j * tk:(j + 1) * tk]
    sgt_ref[0] = _silu(lax.dot_general(wgt_ref[...], hb, _NT, preferred_element_type=F32)).astype(BF16)


def _qkvg_prompt(x1, mod, ng, wqt, wkt, wk, wv, wvt, wgt, tabk, ck, sk, *, tk):
    nb, t, _ = x1.shape
    tb = 512
    kern = functools.partial(_qkvg_prompt_kernel, tb=tb, tk=tk)
    row = pl.BlockSpec((1, tb, D), lambda b, i: (b, i, 0))
    col = pl.BlockSpec((1, D, tb), lambda b, i: (b, 0, i))
    wspec = _const_spec(wqt.shape)
    return pl.pallas_call(
        kern,
        grid=(nb, t // tb),
        in_specs=[row, _const_spec(mod.shape), _const_spec(ng.shape)] + [wspec] * 6 + [
            pl.BlockSpec((3, tb, 128), lambda b, i: (0, i, 0)),
            pl.BlockSpec((ROT_DIM // 2, tb), lambda b, i: (0, i)),
            pl.BlockSpec((ROT_DIM // 2, tb), lambda b, i: (0, i))],
        out_specs=[col,
                   pl.BlockSpec((1, 2 * N_HEADS, HEAD_DIM, tb), lambda b, i: (b, 0, 0, i)),
                   row, row,
                   pl.BlockSpec((1, tb // tk, D, tk), lambda b, i: (b, i, 0, 0)),
                   col],
        out_shape=[jax.ShapeDtypeStruct((nb, D, t), BF16),
                   jax.ShapeDtypeStruct((nb, 2 * N_HEADS, HEAD_DIM, t), F32),
                   jax.ShapeDtypeStruct((nb, t, D), BF16),
                   jax.ShapeDtypeStruct((nb, t, D), F32),
                   jax.ShapeDtypeStruct((nb, t // tk, D, tk), BF16),
                   jax.ShapeDtypeStruct((nb, D, t), BF16)],
        compiler_params=pltpu.CompilerParams(dimension_semantics=("arbitrary", "arbitrary"),
                                             vmem_limit_bytes=VMEM_LIMIT),
        name="qkvg_prompt",
    )(x1, mod, ng, wqt, wkt, wk, wv, wvt, wgt, tabk, ck, sk)


def _qkvg_sample_kernel(x_ref, mod_ref, ng_ref, wq_ref, wk_ref, wv_ref, wg_ref, tq_ref, tk_ref,
                        q_ref, k_ref, v_ref, sg_ref):
    hn = _rms(x_ref[...], ng_ref[...])
    hb = (hn * (1.0 + mod_ref[:, D:2 * D]) + mod_ref[:, 0:D]).astype(BF16)
    for j, qj in enumerate(_rope_lanes(_dot(hb, wq_ref[...]), tq_ref)):
        q_ref[:, 128 * j:128 * (j + 1)] = qj
    for j, kj in enumerate(_rope_lanes(_dot(hb, wk_ref[...]), tk_ref)):
        k_ref[:, 128 * j:128 * (j + 1)] = kj
    v_ref[...] = _dot(hb, wv_ref[...])
    sg_ref[...] = _silu(_dot(hb, wg_ref[...]))


def _qkvg_sample(x1, mod, ng, wq, wk, wv, wg, tabq, tabk):
    n = x1.shape[0]
    return pl.pallas_call(
        _qkvg_sample_kernel,
        out_shape=[jax.ShapeDtypeStruct((n, D), F32)] * 4,
        compiler_params=pltpu.CompilerParams(vmem_limit_bytes=VMEM_LIMIT),
        name="qkvg_sample",
    )(x1, mod, ng, wq, wk, wv, wg, tabq, tabk)


def _diff_lambda(lp):
    e1 = jnp.exp(jnp.sum(lp[0:1] * lp[1:2], axis=-1, keepdims=True))
    e2 = jnp.exp(jnp.sum(lp[2:3] * lp[3:4], axis=-1, keepdims=True))
    return e1 - e2 + LAMBDA_INIT


def _subln(o, sub):
    return _rms(o, sub) * (1.0 - LAMBDA_INIT)


PAGES_PER_ITEM = 8
ITEMS_PER_BLOCK = 2


def _attn_kernel(pt_ref,
                 qt_ref, kb_ref, vt_ref, sgt_ref, x1_ref, mod_ref, lamp_ref, sub_ref, wout_ref, fng_ref,
                 q2t_ref, q2_ref, kn_ref, vn_ref, rep_ref, ck_hbm, cv_hbm,
                 y_ref, os_ref,
                 o_s, qq_s, m_s, l_s, acc_s, st_s,
                 kbuf, vbuf, qb_s, ms_s, ls_s, accs_s, cnt_s, sem_k, sem_v,
                 *, tq, tk, n_items, steps_per_seq):
    b = pl.program_id(0)
    qi = pl.program_id(1)
    g = PAGES_PER_ITEM
    nr = 2 * N_HEADS

    def fetch(item, slot, start):
        sb = item // steps_per_seq
        sp = item % steps_per_seq
        for i in range(g):
            pid = pt_ref[sb, sp * g + i]
            for cp in (pltpu.make_async_copy(ck_hbm.at[pid], kbuf.at[slot, i], sem_k.at[slot]),
                       pltpu.make_async_copy(cv_hbm.at[pid], vbuf.at[slot, i], sem_v.at[slot])):
                if start:
                    cp.start()
                else:
                    cp.wait()

    def compute_item(item, slot):
        sb = item // steps_per_seq
        sp = item % steps_per_seq

        @pl.when(sp == 0)
        def _():
            qcol = q2t_ref[sb]
            for n in range(nr):
                qb_s[n // N_HEADS, n % N_HEADS] = jnp.broadcast_to(qcol[:, n:n + 1], (HEAD_DIM, PAGE_SIZE))
            ms_s[...] = jnp.sum(q2_ref[sb] * kn_ref[sb], axis=-1, keepdims=True)
            ls_s[...] = jnp.ones((nr, 1), F32)
            accs_s[...] = jnp.concatenate([vn_ref[sb], vn_ref[sb]], axis=0)

        rows = []
        for i in range(g):
            for c in range(2):
                rows.append(jnp.sum(kbuf[slot, i, :, c] * qb_s[c], axis=1))
        s = jnp.concatenate(rows, axis=0)
        m_blk = jnp.max(jnp.max(s, axis=-1, keepdims=True).reshape(g, nr, 1), axis=0)
        m_old = ms_s[...]
        m_new = jnp.maximum(m_old, m_blk)
        corr = jnp.exp(m_old - m_new)
        pr = jnp.exp(s - jnp.tile(m_new, (g, 1)))
        ls_s[...] = ls_s[...] * corr + jnp.sum(jnp.sum(pr, axis=-1, keepdims=True).reshape(g, nr, 1), axis=0)
        ms_s[...] = m_new
        pexp = _dot(pr.astype(BF16), rep_ref[...])
        lane = lax.broadcasted_iota(jnp.int32, pexp.shape, 1)
        row = lax.broadcasted_iota(jnp.int32, pexp.shape, 0)
        pexp = jnp.where((lane & (N_HEADS - 1)) == (row & (N_HEADS - 1)), pexp, 0.0).astype(BF16)
        acc = accs_s[...] * corr
        for i in range(g):
            v2 = vbuf[slot, i].reshape(PAGE_SIZE * N_HEADS, V_DIM).astype(BF16)
            acc = acc + _dot(pexp[nr * i:nr * (i + 1)], v2)
        accs_s[...] = acc

        @pl.when(sp == steps_per_seq - 1)
        def _():
            o12 = accs_s[...] / ls_s[...]
            os_ref[sb] = o12[:N_HEADS] - _diff_lambda(lamp_ref[...]) * o12[N_HEADS:]

    def item_slot():
        item = cnt_s[0]

        @pl.when(item < n_items)
        def _():
            slot = item & 1
            fetch(item, slot, False)
            compute_item(item, slot)

            @pl.when(item + 2 < n_items)
            def _():
                fetch(item + 2, slot, True)

            cnt_s[0] = item + 1

    @pl.when((b == 0) & (qi == 0))
    def _():
        cnt_s[0] = 0
        fetch(0, 0, True)
        fetch(1, 1, True)

    row = lax.broadcasted_iota(jnp.int32, (2 * HEAD_DIM, tq), 0)
    for h in range(N_HEADS):
        qh = qt_ref[0, V_DIM * h:V_DIM * (h + 1), :]
        zero = jnp.zeros_like(qh)
        qq_s[h, :, 0:tq] = jnp.where(row < HEAD_DIM, qh, zero)
        qq_s[h, :, tq:2 * tq] = jnp.where(row >= HEAD_DIM, qh, zero)
    m_s[...] = jnp.full(m_s.shape, NEG, F32)
    l_s[...] = jnp.zeros(l_s.shape, F32)
    acc_s[...] = jnp.zeros(acc_s.shape, F32)

    def scores(j, h):
        hs = slice(V_DIM * h, V_DIM * (h + 1))
        return _dot(kb_ref[0, pl.ds(pl.multiple_of(j * tk, tk), tk), hs], qq_s[h])

    def update(j, h, st, masked):
        if masked:
            keyi = lax.broadcasted_iota(jnp.int32, (tk, 2 * tq), 0)
            qryi = lax.broadcasted_iota(jnp.int32, (tk, 2 * tq), 1) & (tq - 1)
            st = jnp.where(keyi <= qryi, st, NEG)
        m_old = m_s[h]
        m_new = jnp.maximum(m_old, jnp.max(st, axis=0, keepdims=True))
        corr = jnp.exp2(m_old - m_new)
        p = jnp.exp2(st - m_new)
        m_s[h] = m_new
        l_s[h] = l_s[h] * corr + jnp.sum(p, axis=0, keepdims=True)
        acc_s[h] = acc_s[h] * corr + _dot(vt_ref[0, j, V_DIM * h:V_DIM * (h + 1), :], p.astype(BF16))

    for h in range(N_HEADS):
        st_s[h] = scores(0, h)

    def body(j, carry):
        for _ in range(ITEMS_PER_BLOCK):
            item_slot()
        for h in range(N_HEADS):
            cur = st_s[h]
            nxt = scores(j + 1, h)
            update(j, h, cur, False)
            st_s[h] = nxt
        return carry

    lax.fori_loop(0, qi, body, 0)
    for _ in range(ITEMS_PER_BLOCK):
        item_slot()
    for h in range(N_HEADS):
        update(qi, h, st_s[h], True)

    lam = _diff_lambda(lamp_ref[...])
    for h in range(N_HEADS):
        hs = slice(V_DIM * h, V_DIM * (h + 1))
        o12 = acc_s[h] / l_s[h]
        ot = o12[:, :tq] - lam * o12[:, tq:]
        ms = jnp.mean(ot * ot, axis=0, keepdims=True)
        on = ot * lax.rsqrt(ms + EPS) * sub_ref[...] * (1.0 - LAMBDA_INIT)
        o_s[hs, :] = (on * sgt_ref[0, hs, :].astype(F32)).astype(BF16)
    out = lax.dot_general(o_s[...], wout_ref[...], (((0,), (0,)), ((), ())), preferred_element_type=F32)
    x2 = x1_ref[0] + mod_ref[pl.ds(b, 1), 2 * D:3 * D] * out
    y_ref[0] = _rms(x2, fng_ref[...])


def _attn(qt, kb, vtb, sgt, x1, mod, lamp, sub, wout, fng, page_table, q2, kn, vn, ck5, cv, *, tq):
    nb, t, _ = kb.shape
    tk = vtb.shape[-1]
    ns, n_pages = page_table.shape
    g = PAGES_PER_ITEM
    steps_per_seq = n_pages // g
    n_items = ns * steps_per_seq
    n_blocks = nb * sum(range(1, t // tq + 1))
    assert n_pages % g == 0 and 2 <= n_items <= ITEMS_PER_BLOCK * n_blocks
    nr = 2 * N_HEADS
    rep = (jnp.arange(PAGE_SIZE * N_HEADS)[None, :] // N_HEADS == jnp.arange(PAGE_SIZE)[:, None]).astype(BF16)
    q2 = q2.reshape(ns, nr, HEAD_DIM)
    kn = kn.reshape(ns, nr, HEAD_DIM)
    kern = functools.partial(_attn_kernel, tq=tq, tk=tk, n_items=n_items, steps_per_seq=steps_per_seq)
    row = pl.BlockSpec((1, tq, D), lambda b, i, pt: (b, i, 0))
    col = pl.BlockSpec((1, D, tq), lambda b, i, pt: (b, 0, i))

    def const(shape):
        nd = len(shape)
        return pl.BlockSpec(shape, lambda b, i, pt: (0,) * nd, pipeline_mode=pl.Buffered(1))

    grid_spec = pltpu.PrefetchScalarGridSpec(
        num_scalar_prefetch=1,
        grid=(nb, t // tq),
        in_specs=[col,
                  pl.BlockSpec((1, t, D), lambda b, i, pt: (b, 0, 0)),
                  pl.BlockSpec((1, t // tk, D, tk), lambda b, i, pt: (b, 0, 0, 0)),
                  col, row, const(mod.shape), const(lamp.shape), const(sub.shape),
                  const(wout.shape), const(fng.shape),
                  const((ns, HEAD_DIM, nr)), const(q2.shape), const(kn.shape), const(vn.shape), const(rep.shape),
                  pl.BlockSpec(memory_space=pl.ANY), pl.BlockSpec(memory_space=pl.ANY)],
        out_specs=[row, pl.BlockSpec((ns, N_HEADS, V_DIM), lambda b, i, pt: (0, 0, 0))],
        scratch_shapes=[pltpu.VMEM((D, tq), BF16),
                        pltpu.VMEM((N_HEADS, 2 * HEAD_DIM, 2 * tq), BF16),
                        pltpu.VMEM((N_HEADS, 1, 2 * tq), F32),
                        pltpu.VMEM((N_HEADS, 1, 2 * tq), F32),
                        pltpu.VMEM((N_HEADS, V_DIM, 2 * tq), F32),
                        pltpu.VMEM((N_HEADS, tk, 2 * tq), F32),
                        pltpu.VMEM((2, g, N_HEADS, 2, HEAD_DIM, PAGE_SIZE), F32),
                        pltpu.VMEM((2, g, PAGE_SIZE, N_HEADS, V_DIM), F32),
                        pltpu.VMEM((2, N_HEADS, HEAD_DIM, PAGE_SIZE), F32),
                        pltpu.VMEM((nr, 1), F32), pltpu.VMEM((nr, 1), F32), pltpu.VMEM((nr, V_DIM), F32),
                        pltpu.SMEM((1,), jnp.int32),
                        pltpu.SemaphoreType.DMA((2,)), pltpu.SemaphoreType.DMA((2,))],
    )
    return pl.pallas_call(
        kern,
        grid_spec=grid_spec,
        out_shape=[jax.ShapeDtypeStruct((nb, t, D), F32), jax.ShapeDtypeStruct((ns, N_HEADS, V_DIM), F32)],
        compiler_params=pltpu.CompilerParams(dimension_semantics=("arbitrary", "arbitrary"),
                                             vmem_limit_bytes=VMEM_LIMIT),
        name="attn",
    )(page_table, qt, kb, vtb, sgt, x1, mod, lamp, sub, wout, fng,
      q2.transpose(0, 2, 1), q2, kn, vn, rep, ck5, cv)


def _epi_sample_kernel(o_ref, sg_ref, x1_ref, mod_ref, sub_ref, wout_ref, fng_ref, y_ref):
    parts = []
    for h in range(N_HEADS):
        hs = slice(V_DIM * h, V_DIM * (h + 1))
        parts.append((_subln(o_ref[:, hs], sub_ref[...]) * sg_ref[:, hs]).astype(BF16))
    out = _dot(jnp.concatenate(parts, axis=1), wout_ref[...])
    x2 = x1_ref[...] + mod_ref[:, 2 * D:3 * D] * out
    y_ref[...] = _rms(x2, fng_ref[...])


def _epi_sample(o, sg, x1, mod, sub, wout, fng):
    return pl.pallas_call(
        _epi_sample_kernel,
        out_shape=jax.ShapeDtypeStruct(x1.shape, F32),
        name="epi_sample",
    )(o, sg, x1, mod, sub, wout, fng)


def _block_diag_slices(w):
    per = LRU_HEADS // N_SLICE
    w4 = w.reshape(N_SLICE, per, LRU_BLK, LRU_BLK)
    eye = jnp.eye(per, dtype=w.dtype)
    return jnp.einsum('skcd,kl->skcld', w4, eye).reshape(N_SLICE, SLICE_W, SLICE_W)


def _sliced(p):
    return p.reshape(p.shape[0], N_SLICE, SLICE_W).transpose(1, 0, 2)


def _l0_weights(w_in, conv_a_w, conv_a_b, wr, br, wi, bi, lam, conv_b_w, w_out):
    win = w_in.astype(BF16).reshape(D, 6, N_SLICE, SLICE_W).transpose(2, 0, 1, 3).reshape(N_SLICE, D, 6 * SLICE_W)
    wg = jnp.concatenate([_block_diag_slices(wr), _block_diag_slices(wi)], axis=-1).astype(BF16)
    wout = w_out.astype(BF16).reshape(2, N_SLICE, SLICE_W, D).transpose(1, 0, 2, 3).reshape(N_SLICE, 2 * SLICE_W, D)
    return (win, _sliced(conv_a_w), _sliced(conv_a_b[None]), wg, _sliced(br[None]), _sliced(bi[None]),
            _sliced(lam[None]), _sliced(conv_b_w), wout)


def _rope_lane_tables(pos, scale):
    half = ROT_DIM // 2
    inv = jnp.power(jnp.float32(ROPE_THETA), -jnp.arange(0, ROT_DIM, 2, dtype=F32) / ROT_DIM)
    ang = pos.astype(F32)[:, None] * inv[None, :]
    lane = jnp.arange(128)
    within = lane % HEAD_DIM
    cos = jnp.cos(ang)[:, lane % half]
    sin = jnp.sin(ang)[:, lane % half]
    c = jnp.where(within < ROT_DIM, cos, 1.0)
    sa = jnp.where((within >= half) & (within < ROT_DIM), sin, 0.0)
    sb = jnp.where(within < half, -sin, 0.0)
    return jnp.stack([c, sa, sb]) * scale


def kernel(x_prompt, x_sample, state_lru_h, state_conv_a, state_conv_b, cache_k, cache_v, page_table,
           c_prompt, c_sample, l0_norm_g, l0_ada_w, l0_ada_b, l0_w_in, l0_conv_a_w, l0_conv_a_b,
           l0_lru_wr, l0_lru_br, l0_lru_wi, l0_lru_bi, l0_lru_lam, l0_conv_b_w, l0_w_out,
           l1_norm_g, l1_ada_w, l1_ada_b, l1_w_in, l1_lam_q1, l1_lam_k1, l1_lam_q2, l1_lam_k2,
           l1_subln_g, l1_w_out, final_norm_g):
    nb, t, _ = x_prompt.shape
    ns = x_sample.shape[0]
    n_pages = page_table.shape[1]
    past_len = n_pages * PAGE_SIZE
    tq = 256

    c_all = jnp.concatenate([c_prompt, c_sample], axis=0)
    mod0 = _ada(c_all, l0_ada_w, l0_ada_b)
    mod1 = _ada(c_all, l1_ada_w, l1_ada_b)

    w0 = _l0_weights(l0_w_in, l0_conv_a_w, l0_conv_a_b, l0_lru_wr, l0_lru_br, l0_lru_wi, l0_lru_bi,
                     l0_lru_lam, l0_conv_b_w, l0_w_out)
    ng0 = l0_norm_g.reshape(1, D)
    x1p, hl_p, ca_p, cb_p = _l0_prompt(x_prompt, mod0[:nb], ng0, w0)
    x1s, hl_s, ca_s, cb_s = _l0_sample(x_sample.reshape(ns, D), mod0[nb:], ng0, state_lru_h,
                                       state_conv_a.transpose(1, 0, 2), state_conv_b.transpose(1, 0, 2), w0)
    lru_h_p = hl_p.transpose(1, 0, 2).reshape(nb, D)
    conv_a_p = ca_p.transpose(1, 2, 0, 3).reshape(nb, CONV_A - 1, D)
    conv_b_p = cb_p.transpose(1, 2, 0, 3).reshape(nb, CONV_B - 1, D)
    conv_a_s = ca_s.transpose(1, 0, 2)
    conv_b_s = cb_s.transpose(1, 0, 2)

    w1 = l1_w_in.astype(BF16)
    wq, wk, wv, wg = w1[:, 0:D], w1[:, D:2 * D], w1[:, 2 * D:3 * D], w1[:, 3 * D:4 * D]
    ng1 = l1_norm_g.reshape(1, D)
    pos_p = jnp.arange(t, dtype=jnp.int32)
    pos_s = jnp.full((1,), past_len, jnp.int32)
    half = ROT_DIM // 2
    inv = jnp.power(jnp.float32(ROPE_THETA), -jnp.arange(0, ROT_DIM, 2, dtype=F32) / ROT_DIM)
    ang_t = inv[:, None] * pos_p.astype(F32)[None, :]
    qt_p, kt_p, kb_p, v_p, vtb_p, sgt_p = _qkvg_prompt(
        x1p, mod1[:nb], ng1, wq.T, wk.T, wk, wv, wv.T, wg.T, _rope_lane_tables(pos_p, 1.0),
        jnp.cos(ang_t), jnp.sin(ang_t), tk=tq)
    q_s, k_s, v_s, sg_s = _qkvg_sample(x1s, mod1[nb:], ng1, wq, wk, wv, wg,
                                       _rope_lane_tables(pos_s, SCALE), _rope_lane_tables(pos_s, 1.0))

    lamp = jnp.stack([l1_lam_q1, l1_lam_k1, l1_lam_q2, l1_lam_k2])
    sub = l1_subln_g.reshape(1, V_DIM)
    wout1 = l1_w_out.astype(BF16)
    fng = final_norm_g.reshape(1, D)
    q2 = q_s.reshape(ns, N_HEADS, 2, HEAD_DIM).transpose(0, 2, 1, 3)
    kn = k_s.reshape(ns, N_HEADS, 2, HEAD_DIM).transpose(0, 2, 1, 3)
    ck5 = cache_k.transpose(0, 2, 3, 1).reshape(cache_k.shape[0], N_HEADS, 2, HEAD_DIM, PAGE_SIZE)
    y_p, o_s = _attn(qt_p, kb_p, vtb_p, sgt_p, x1p, mod1[:nb], lamp, l1_subln_g.reshape(V_DIM, 1), wout1, fng,
                     page_table, q2, kn, v_s.reshape(ns, N_HEADS, V_DIM), ck5, cache_v, tq=tq)
    y_s = _epi_sample(o_s.reshape(ns, D), sg_s, x1s, mod1[nb:], sub, wout1, fng)

    k_p = kt_p.transpose(0, 3, 1, 2)
    return (y_p, y_s.reshape(ns, 1, D), lru_h_p, hl_s, conv_a_p, conv_a_s, conv_b_p, conv_b_s,
            k_p, v_p.reshape(nb, t, N_HEADS, V_DIM), k_s.reshape(ns, 1, 2 * N_HEADS, HEAD_DIM),
            v_s.reshape(ns, 1, N_HEADS, V_DIM))
```
